```python
import math
import jax, jax.numpy as jnp
from jax import lax
import numpy as np

D_MODEL = 1024
BATCH = 2
SEQ = 8192
DEPTH = 4
DEC_BATCH = 128
DEC_SEQ = 1
PAST_LEN = 8192
PAGE_SIZE = 128

HEAD_DIM = 64
A_HEADS = 4
A_KV_HEADS = 2
MOBA_BLOCK = 256
MOBA_TOPK = 3
B_HEADS = 8
B_Q_RANK = 256
B_KV_RANK = 128
B_NOPE = 64
B_ROPE = 32
B_V = 64
ROPE_THETA = 10000.0
C_HEADS = 4
CMP_STRIDE = 16
CMP_BLOCK = 2 * CMP_STRIDE
CMP_HIDDEN = 128
SEL_BLOCK = 64
SEL_TOPK = 16
WINDOW = 512
NUM_BUCKETS = 32
MAX_DISTANCE = 128
D_FF = 2816
Q_BLOCK = 128
ALPHA = (2 * DEPTH) ** 0.25
BETA = (8 * DEPTH) ** -0.25
FORCE_SCORE = 1e9
MIX_WIDTH = A_HEADS * HEAD_DIM + B_HEADS * B_V + C_HEADS * HEAD_DIM
IN_SIZES = (A_HEADS * HEAD_DIM, A_KV_HEADS * HEAD_DIM, A_KV_HEADS * HEAD_DIM, B_Q_RANK, B_KV_RANK, B_ROPE, C_HEADS * HEAD_DIM, 2 * HEAD_DIM, 2 * HEAD_DIM, 2 * HEAD_DIM, 3 * C_HEADS)
IN_COLS = sum(IN_SIZES)
IN_SPLITS = tuple(int(s) for s in np.cumsum(IN_SIZES)[:-1])

kernel_name = 'hybrid_moba_mla_nsa_decoder_step'


def layer_norm(x, g, b, eps=1e-5):
    xf = x.astype(jnp.float32)
    mu = xf.mean(-1, keepdims=True)
    var = jnp.mean(jnp.square(xf - mu), -1, keepdims=True)
    return ((xf - mu) * lax.rsqrt(var + eps) * g + b).astype(x.dtype)


def rms_norm(x, g, eps=1e-6):
    xf = x.astype(jnp.float32)
    return (xf * lax.rsqrt(jnp.mean(xf * xf, -1, keepdims=True) + eps) * g).astype(x.dtype)


def swiglu(x, wg, wu, wd):
    return (jax.nn.silu(x @ wg) * (x @ wu)) @ wd


def macaron_half(x, wg, wu, wd, g, b):
    return layer_norm(ALPHA * x + 0.5 * swiglu(x, wg, wu, wd), g, b)


def masked_softmax(s, mask):
    s = jnp.where(mask, s.astype(jnp.float32), -jnp.inf)
    m = jnp.max(s, -1, keepdims=True)
    m = jnp.where(jnp.isfinite(m), m, 0.0)
    e = jnp.exp(s - m)
    return e / jnp.maximum(e.sum(-1, keepdims=True), 1e-30)


def t5_bucket(dist):
    n = jnp.maximum(dist, 0)
    max_exact = NUM_BUCKETS // 2
    nf = jnp.maximum(n, 1).astype(jnp.float32)
    large = max_exact + (jnp.log(nf / max_exact) / math.log(MAX_DISTANCE / max_exact) * (NUM_BUCKETS - max_exact)).astype(jnp.int32)
    return jnp.where(n < max_exact, n, jnp.minimum(large, NUM_BUCKETS - 1))


def rope_angles(pos):
    half = B_ROPE // 2
    inv = jnp.power(ROPE_THETA, -jnp.arange(half, dtype=jnp.float32) / half)
    return pos.astype(jnp.float32)[:, None] * inv


def apply_rope(x, ang):
    half = x.shape[-1] // 2
    c, s = jnp.cos(ang), jnp.sin(ang)
    x1, x2 = x[..., :half].astype(jnp.float32), x[..., half:].astype(jnp.float32)
    return jnp.concatenate([x1 * c - x2 * s, x1 * s + x2 * c], -1).astype(x.dtype)


def gather_rows(cache, layer, pt, pos):
    b = jnp.arange(pt.shape[0]).reshape((-1,) + (1,) * (pos.ndim - 1))
    return cache[layer, pt[b, pos // PAGE_SIZE], pos % PAGE_SIZE]


def project(h, pos, w_in_l, q_norm, w_qb, kv_norm, w_uk):
    B_, S, _ = h.shape
    a_q, a_k, a_v, b_qc, b_kvc, b_kpe, c_q, c_cmp, c_sel, c_win, c_gate = jnp.split(h @ w_in_l, IN_SPLITS, axis=-1)
    ang = rope_angles(pos)
    qb = jnp.einsum('bsr,rhd->bshd', rms_norm(b_qc, q_norm), w_qb)
    q_lat = jnp.einsum('bshd,rhd->bshr', qb[..., :B_NOPE], w_uk)
    q_pe = apply_rope(qb[..., B_NOPE:], ang[:, None, :])
    moba = (a_q.reshape(B_, S, A_HEADS, HEAD_DIM), a_k.reshape(B_, S, A_KV_HEADS, HEAD_DIM), a_v.reshape(B_, S, A_KV_HEADS, HEAD_DIM))
    mla = (q_lat, q_pe, rms_norm(b_kvc, kv_norm), apply_rope(b_kpe, ang))
    nsa = (c_q.reshape(B_, S, C_HEADS, HEAD_DIM), c_cmp.reshape(B_, S, 2, HEAD_DIM), c_sel.reshape(B_, S, 2, HEAD_DIM), c_win.reshape(B_, S, 2, HEAD_DIM), jax.nn.sigmoid(c_gate.reshape(B_, S, C_HEADS, 3)))
    return moba, mla, nsa


def moba_prompt(q, k, v, tab):
    B_, S, H, D = q.shape
    G = H // A_KV_HEADS
    scale = D ** -0.5
    nb = -(-S // MOBA_BLOCK)
    pad = ((0, 0), (0, nb * MOBA_BLOCK - S), (0, 0), (0, 0))
    kp, vp = jnp.pad(k, pad), jnp.pad(v, pad)
    kb = kp.reshape(B_, nb, MOBA_BLOCK, A_KV_HEADS, D).transpose(0, 3, 1, 2, 4)
    vb = vp.reshape(B_, nb, MOBA_BLOCK, A_KV_HEADS, D).transpose(0, 3, 1, 2, 4)
    kmean = jnp.repeat(kb.astype(jnp.float32).mean(3), G, axis=1)
    k_top = min(MOBA_TOPK, nb)
    bi = jnp.arange(B_).reshape(B_, 1, 1, 1)
    kvh = (jnp.arange(H) // G).reshape(1, H, 1, 1)
    h5 = jnp.arange(H).reshape(1, H, 1, 1, 1)
    blk = jnp.arange(nb)

    def chunk(c):
        q0 = c * Q_BLOCK
        qc = lax.dynamic_slice_in_dim(q, q0, Q_BLOCK, 1)
        tq = q0 + jnp.arange(Q_BLOCK)
        gate = jnp.einsum('bqhd,bhnd->bhqn', qc.astype(jnp.float32), kmean)
        gate = jnp.where(blk < (tq // MOBA_BLOCK)[:, None], gate, -jnp.inf)
        top_v, top_i = lax.top_k(gate, k_top)
        ks = kb[bi, kvh, top_i]
        vs = vb[bi, kvh, top_i]
        kpos = top_i[..., None] * MOBA_BLOCK + jnp.arange(MOBA_BLOCK)
        s_s = jnp.einsum('bqhd,bhqnkd->bhqnk', qc, ks) * scale + tab[h5, t5_bucket(tq[:, None, None] - kpos)]
        m_s = jnp.broadcast_to(jnp.isfinite(top_v)[..., None], s_s.shape)
        own0 = (q0 // MOBA_BLOCK) * MOBA_BLOCK
        ko = jnp.repeat(lax.dynamic_slice_in_dim(kp, own0, MOBA_BLOCK, 1), G, axis=2)
        vo = jnp.repeat(lax.dynamic_slice_in_dim(vp, own0, MOBA_BLOCK, 1), G, axis=2)
        do = tq[:, None] - (own0 + jnp.arange(MOBA_BLOCK))[None, :]
        s_o = jnp.einsum('bqhd,bkhd->bhqk', qc, ko) * scale + tab[:, t5_bucket(do)]
        n_s = k_top * MOBA_BLOCK
        p = masked_softmax(jnp.concatenate([s_s.reshape(B_, H, Q_BLOCK, n_s), s_o], -1),
                           jnp.concatenate([m_s.reshape(B_, H, Q_BLOCK, n_s), jnp.broadcast_to(do >= 0, s_o.shape)], -1)).astype(v.dtype)
        return (jnp.einsum('bhqk,bhqkd->bqhd', p[..., :n_s], vs.reshape(B_, H, Q_BLOCK, n_s, D))
                + jnp.einsum('bhqk,bkhd->bqhd', p[..., n_s:], vo))

    o = lax.map(chunk, jnp.arange(S // Q_BLOCK))
    return o.transpose(1, 0, 2, 3, 4).reshape(B_, S, H * D)


def moba_sample(q, k_new, v_new, cache, layer, pt, tab):
    DB, DS, H, D = q.shape
    G = H // A_KV_HEADS
    scale = D ** -0.5
    past = pt.shape[1] * PAGE_SIZE
    tq = past + jnp.arange(DS)
    n_full = past // MOBA_BLOCK
    own0 = n_full * MOBA_BLOCK
    own_past = gather_rows(cache, layer, pt, jnp.arange(own0, past)[None])
    ko = jnp.repeat(jnp.concatenate([own_past[:, :, 0], k_new], 1), G, axis=2)
    vo = jnp.repeat(jnp.concatenate([own_past[:, :, 1], v_new], 1), G, axis=2)
    do = tq[:, None] - (own0 + jnp.arange(ko.shape[1]))[None, :]
    s_o = jnp.einsum('bqhd,bkhd->bhqk', q, ko) * scale + tab[:, t5_bucket(do)]
    m_o = jnp.broadcast_to(do >= 0, s_o.shape)
    if n_full == 0:
        p = masked_softmax(s_o, m_o).astype(vo.dtype)
        return jnp.einsum('bhqk,bkhd->bqhd', p, vo).reshape(DB, DS, H * D)
    pos_f = jnp.arange(own0)[None]
    k_full = cache[layer, pt[jnp.arange(DB)[:, None], pos_f // PAGE_SIZE], pos_f % PAGE_SIZE, 0]
    kmean = jnp.repeat(k_full.astype(jnp.float32).reshape(DB, n_full, MOBA_BLOCK, A_KV_HEADS, D).mean(2), G, axis=2)
    gate = jnp.einsum('bqhd,bnhd->bhqn', q.astype(jnp.float32), kmean)
    _, top_i = lax.top_k(gate, min(MOBA_TOPK, n_full))
    kpos = top_i[..., None] * MOBA_BLOCK + jnp.arange(MOBA_BLOCK)
    b5 = jnp.arange(DB).reshape(DB, 1, 1, 1, 1)
    h5 = jnp.arange(H).reshape(1, H, 1, 1, 1)
    rows = cache[layer, pt[b5, kpos // PAGE_SIZE], kpos % PAGE_SIZE, :, h5 // G]
    s_s = jnp.einsum('bqhd,bhqnkd->bhqnk', q, rows[..., 0, :]) * scale + tab[h5, t5_bucket(tq[:, None, None] - kpos)]
    n_s = s_s.shape[3] * MOBA_BLOCK
    p = masked_softmax(jnp.concatenate([s_s.reshape(DB, H, DS, n_s), s_o], -1),
                       jnp.concatenate([jnp.ones((DB, H, DS, n_s), bool), m_o], -1)).astype(vo.dtype)
    o = (jnp.einsum('bhqk,bhqkd->bqhd', p[..., :n_s], rows[..., 1, :].reshape(DB, H, DS, n_s, D))
         + jnp.einsum('bhqk,bkhd->bqhd', p[..., n_s:], vo))
    return o.reshape(DB, DS, H * D)


def mla_attend(q_lat, q_pe, c_kv, k_pe, qpos, kpos):
    scale = (B_NOPE + B_ROPE) ** -0.5
    s = (jnp.einsum('bqhr,bkr->bhqk', q_lat, c_kv) + jnp.einsum('bqhd,bkd->bhqk', q_pe, k_pe)) * scale
    p = masked_softmax(s, kpos[None, :] <= qpos[:, None])
    return jnp.einsum('bhqk,bkr->bqhr', p.astype(c_kv.dtype), c_kv)


def mla_prompt(q_lat, q_pe, c_kv, k_pe):
    B_, S, H, R = q_lat.shape
    kpos = jnp.arange(S)

    def chunk(c):
        q0 = c * Q_BLOCK
        return mla_attend(lax.dynamic_slice_in_dim(q_lat, q0, Q_BLOCK, 1), lax.dynamic_slice_in_dim(q_pe, q0, Q_BLOCK, 1),
                          c_kv, k_pe, q0 + jnp.arange(Q_BLOCK), kpos)

    o = lax.map(chunk, jnp.arange(S // Q_BLOCK))
    return o.transpose(1, 0, 2, 3, 4).reshape(B_, S, H, R)


def mla_sample(q_lat, q_pe, c_kv, k_pe, cache, layer, pt):
    past = pt.shape[1] * PAGE_SIZE
    rows = gather_rows(cache, layer, pt, jnp.arange(past)[None])
    ckv = jnp.concatenate([rows[..., :B_KV_RANK], c_kv], 1)
    kpe = jnp.concatenate([rows[..., B_KV_RANK:], k_pe], 1)
    return mla_attend(q_lat, q_pe, ckv, kpe, past + jnp.arange(q_lat.shape[1]), jnp.arange(ckv.shape[1]))


def mla_out(o_lat, w_uv):
    o = jnp.einsum('bqhr,rhd->bqhd', o_lat, w_uv)
    return o.reshape(o.shape[0], o.shape[1], B_HEADS * B_V)


def compress(kv, pos_emb, w1, b1, w2):
    B_, L, D = kv.shape
    n_sub = L // CMP_STRIDE
    sub = kv[:, : n_sub * CMP_STRIDE].reshape(B_, n_sub, CMP_STRIDE, D)
    pe = pos_emb.reshape(2, CMP_STRIDE, D)
    w1r = w1.reshape(2, CMP_STRIDE, D, CMP_HIDDEN)
    hid = (jnp.einsum('bnsd,sdh->bnh', sub[:, :-1] + pe[0], w1r[0])
           + jnp.einsum('bnsd,sdh->bnh', sub[:, 1:] + pe[1], w1r[1]) + b1)
    return jax.nn.silu(hid) @ w2


def compress_kv(cmp, cmp_w):
    pos_emb, w1, b1, w2 = cmp_w
    return (compress(cmp[:, :, 0], pos_emb[0], w1[0], b1[0], w2[0]), compress(cmp[:, :, 1], pos_emb[1], w1[1], b1[1], w2[1]))


def cmp_to_sel(nc, ns):
    i = jnp.arange(nc)[:, None] * CMP_STRIDE
    j = jnp.arange(ns)[None, :] * SEL_BLOCK
    return ((i < j + SEL_BLOCK) & (i + CMP_BLOCK > j)).astype(jnp.float32)


def nsa_prompt(q, cmp, sel, win, gates, cmp_w, tab):
    B_, S, H, D = q.shape
    scale = D ** -0.5
    ck, cv = compress_kv(cmp, cmp_w)
    nc = ck.shape[1]
    c_end = jnp.arange(nc) * CMP_STRIDE + CMP_BLOCK - 1
    ns = S // SEL_BLOCK
    to_sel = cmp_to_sel(nc, ns)
    ksb = sel[:, :, 0].reshape(B_, ns, SEL_BLOCK, D)
    vsb = sel[:, :, 1].reshape(B_, ns, SEL_BLOCK, D)
    k_top = min(SEL_TOPK, ns)
    winp = jnp.pad(win, ((0, 0), (WINDOW, 0), (0, 0), (0, 0)))
    bi = jnp.arange(B_)[:, None, None]
    h5 = jnp.arange(H).reshape(1, H, 1, 1, 1)
    blk = jnp.arange(ns)

    def chunk(c):
        q0 = c * Q_BLOCK
        qc = lax.dynamic_slice_in_dim(q, q0, Q_BLOCK, 1)
        g = lax.dynamic_slice_in_dim(gates, q0, Q_BLOCK, 1)
        tq = q0 + jnp.arange(Q_BLOCK)
        dc = tq[:, None] - c_end[None, :]
        s_c = jnp.einsum('bqhd,bnd->bhqn', qc, ck) * scale + tab[:, t5_bucket(dc)]
        p_c = masked_softmax(s_c, dc >= 0)
        o_c = jnp.einsum('bhqn,bnd->bqhd', p_c.astype(cv.dtype), cv)
        cur = (tq // SEL_BLOCK)[:, None]
        imp = jnp.einsum('bhqn,nj->bqj', p_c, to_sel)
        imp = jnp.where((blk == 0) | (blk == cur) | (blk == cur - 1), FORCE_SCORE, imp)
        imp = jnp.where(blk <= cur, imp, -jnp.inf)
        top_v, top_i = lax.top_k(imp, k_top)
        kg = ksb[bi, top_i]
        vg = vsb[bi, top_i]
        kpos = top_i[..., None] * SEL_BLOCK + jnp.arange(SEL_BLOCK)
        dsel = tq[None, :, None, None] - kpos
        s_s = jnp.einsum('bqhd,bqnkd->bhqnk', qc, kg) * scale + tab[h5, t5_bucket(dsel)[:, None]]
        m_s = (jnp.isfinite(top_v)[..., None] & (dsel >= 0))[:, None]
        n_s = k_top * SEL_BLOCK
        p_s = masked_softmax(s_s.reshape(B_, H, Q_BLOCK, n_s), m_s.reshape(B_, 1, Q_BLOCK, n_s))
        o_s = jnp.einsum('bhqk,bqkd->bqhd', p_s.astype(vg.dtype), vg.reshape(B_, Q_BLOCK, n_s, D))
        kw = lax.dynamic_slice_in_dim(winp, q0, WINDOW + Q_BLOCK, 1)
        kpos_w = q0 - WINDOW + jnp.arange(WINDOW + Q_BLOCK)
        dw = tq[:, None] - kpos_w[None, :]
        s_w = jnp.einsum('bqhd,bkd->bhqk', qc, kw[:, :, 0]) * scale + tab[:, t5_bucket(dw)]
        p_w = masked_softmax(s_w, (dw >= 0) & (dw <= WINDOW) & (kpos_w >= 0)[None, :])
        o_w = jnp.einsum('bhqk,bkd->bqhd', p_w.astype(kw.dtype), kw[:, :, 1])
        return g[..., 0:1] * o_c + g[..., 1:2] * o_s + g[..., 2:3] * o_w

    o = lax.map(chunk, jnp.arange(S // Q_BLOCK))
    return o.transpose(1, 0, 2, 3, 4).reshape(B_, S, H * D)


def nsa_sample(q, cmp, sel, win, gates, cache_cmp, cache_sel, win_state, layer, pt, cmp_w, tab):
    DB, DS, H, D = q.shape
    scale = D ** -0.5
    past = pt.shape[1] * PAGE_SIZE
    tq = past + jnp.arange(DS)
    h4 = jnp.arange(H).reshape(1, H, 1, 1)
    cmp_all = jnp.concatenate([gather_rows(cache_cmp, layer, pt, jnp.arange(past)[None]), cmp], 1)
    ck, cv = compress_kv(cmp_all, cmp_w)
    nc = ck.shape[1]
    c_end = jnp.arange(nc) * CMP_STRIDE + CMP_BLOCK - 1
    dc = tq[:, None] - c_end[None, :]
    s_c = jnp.einsum('bqhd,bnd->bhqn', q, ck) * scale + tab[:, t5_bucket(dc)]
    p_c = masked_softmax(s_c, dc >= 0)
    o_c = jnp.einsum('bhqn,bnd->bqhd', p_c.astype(cv.dtype), cv)
    ns = past // SEL_BLOCK
    blk = jnp.arange(ns)
    imp = jnp.einsum('bhqn,nj->bqj', p_c, cmp_to_sel(nc, ns))
    imp = jnp.where((blk == 0) | (blk == ns - 1), FORCE_SCORE, imp)
    _, top_i = lax.top_k(imp, min(SEL_TOPK - 1, ns))
    kpos = top_i[..., None] * SEL_BLOCK + jnp.arange(SEL_BLOCK)
    rows = gather_rows(cache_sel, layer, pt, kpos)
    n_s = kpos.shape[2] * SEL_BLOCK
    k_s = jnp.concatenate([rows[..., 0, :].reshape(DB, DS, n_s, D), jnp.broadcast_to(sel[:, None, :, 0], (DB, DS, DS, D))], 2)
    v_s = jnp.concatenate([rows[..., 1, :].reshape(DB, DS, n_s, D), jnp.broadcast_to(sel[:, None, :, 1], (DB, DS, DS, D))], 2)
    kpos_all = jnp.concatenate([kpos.reshape(DB, DS, n_s), jnp.broadcast_to(tq, (DB, DS, DS))], 2)
    dsel = tq[None, :, None] - kpos_all
    s_s = jnp.einsum('bqhd,bqkd->bhqk', q, k_s) * scale + tab[h4, t5_bucket(dsel)[:, None]]
    p_s = masked_softmax(s_s, (dsel >= 0)[:, None])
    o_s = jnp.einsum('bhqk,bqkd->bqhd', p_s.astype(v_s.dtype), v_s)
    win_all = jnp.concatenate([win_state[layer], win], 1)
    wb = win_state.shape[2]
    kpos_w = past - wb + jnp.arange(wb + DS)
    dw = tq[:, None] - kpos_w[None, :]
    s_w = jnp.einsum('bqhd,bkd->bhqk', q, win_all[:, :, 0]) * scale + tab[:, t5_bucket(dw)]
    p_w = masked_softmax(s_w, (dw >= 0) & (dw <= WINDOW))
    o_w = jnp.einsum('bhqk,bkd->bqhd', p_w.astype(win_all.dtype), win_all[:, :, 1])
    o = gates[..., 0:1] * o_c + gates[..., 1:2] * o_s + gates[..., 2:3] * o_w
    return o.reshape(DB, DS, H * D), win_all[:, -wb:]


def setup_inputs(seed: int = 0) -> dict:
    key = jax.random.key(seed)
    ks = jax.random.split(key, 24)
    f32 = jnp.float32

    def nrm(k, shape, scale):
        return jax.random.normal(k, shape, f32) * scale

    n_pages = PAST_LEN // PAGE_SIZE
    n_used = DEC_BATCH * n_pages
    n_pool = (n_used * 5) // 4
    win_buf = min(WINDOW, PAST_LEN)
    return {
        'x_prompt': nrm(ks[0], (BATCH, SEQ, D_MODEL), 1.0),
        'x_sample': nrm(ks[1], (DEC_BATCH, DEC_SEQ, D_MODEL), 1.0),
        'cache_moba_kv': nrm(ks[2], (DEPTH, n_pool, PAGE_SIZE, 2, A_KV_HEADS, HEAD_DIM), 1.0),
        'cache_mla': nrm(ks[3], (DEPTH, n_pool, PAGE_SIZE, B_KV_RANK + B_ROPE), 1.0),
        'cache_nsa_cmp_kv': nrm(ks[4], (DEPTH, n_pool, PAGE_SIZE, 2, HEAD_DIM), 1.0),
        'cache_nsa_sel_kv': nrm(ks[5], (DEPTH, n_pool, PAGE_SIZE, 2, HEAD_DIM), 1.0),
        'state_nsa_win_kv': nrm(ks[6], (DEPTH, DEC_BATCH, win_buf, 2, HEAD_DIM), 1.0),
        'page_table': jax.random.permutation(ks[7], n_pool)[:n_used].reshape(DEC_BATCH, n_pages).astype(jnp.int32),
        'w_in': nrm(ks[8], (DEPTH, D_MODEL, IN_COLS), D_MODEL ** -0.5),
        'w_out': nrm(ks[9], (DEPTH, MIX_WIDTH, D_MODEL), BETA * MIX_WIDTH ** -0.5),
        'ffn_w_gate': nrm(ks[10], (DEPTH, 2, D_MODEL, D_FF), D_MODEL ** -0.5),
        'ffn_w_up': nrm(ks[11], (DEPTH, 2, D_MODEL, D_FF), D_MODEL ** -0.5),
        'ffn_w_down': nrm(ks[12], (DEPTH, 2, D_FF, D_MODEL), BETA * D_FF ** -0.5),
        'ln_g': 1.0 + nrm(ks[13], (DEPTH, 3, D_MODEL), 0.02),
        'ln_b': nrm(ks[14], (DEPTH, 3, D_MODEL), 0.02),
        'mla_q_norm': 1.0 + nrm(ks[15], (DEPTH, B_Q_RANK), 0.02),
        'mla_w_qb': nrm(ks[16], (DEPTH, B_Q_RANK, B_HEADS, B_NOPE + B_ROPE), B_Q_RANK ** -0.5),
        'mla_kv_norm': 1.0 + nrm(ks[17], (DEPTH, B_KV_RANK), 0.02),
        'mla_w_uk': nrm(ks[18], (DEPTH, B_KV_RANK, B_HEADS, B_NOPE), B_KV_RANK ** -0.5),
        'mla_w_uv': nrm(ks[19], (DEPTH, B_KV_RANK, B_HEADS, B_V), B_KV_RANK ** -0.5),
        'nsa_cmp_pos': nrm(ks[20], (DEPTH, 2, CMP_BLOCK, HEAD_DIM), 0.1),
        'nsa_cmp_w1': nrm(ks[21], (DEPTH, 2, CMP_BLOCK * HEAD_DIM, CMP_HIDDEN), (CMP_BLOCK * HEAD_DIM) ** -0.5),
        'nsa_cmp_b1': nrm(ks[22], (DEPTH, 2, CMP_HIDDEN), 0.02),
        'nsa_cmp_w2': nrm(jax.random.fold_in(ks[23], 1), (DEPTH, 2, CMP_HIDDEN, HEAD_DIM), CMP_HIDDEN ** -0.5),
        'rel_bias': nrm(jax.random.fold_in(ks[23], 2), (NUM_BUCKETS, A_HEADS + C_HEADS), 0.5),
    }


def reference(x_prompt, x_sample, cache_moba_kv, cache_mla, cache_nsa_cmp_kv, cache_nsa_sel_kv, state_nsa_win_kv, page_table,
              w_in, w_out, ffn_w_gate, ffn_w_up, ffn_w_down, ln_g, ln_b, mla_q_norm, mla_w_qb, mla_kv_norm, mla_w_uk, mla_w_uv,
              nsa_cmp_pos, nsa_cmp_w1, nsa_cmp_b1, nsa_cmp_w2, rel_bias):
    tab_a = rel_bias[:, :A_HEADS].T
    tab_c = rel_bias[:, A_HEADS:].T
    past = page_table.shape[1] * PAGE_SIZE
    seq = x_prompt.shape[1]
    pos_p = jnp.arange(seq, dtype=jnp.int32)
    pos_s = past + jnp.arange(x_sample.shape[1], dtype=jnp.int32)
    xp, xs = x_prompt, x_sample
    moba_p, moba_s, mla_p, mla_s, cmp_p, cmp_s, sel_p, sel_s, win_p, win_s = [], [], [], [], [], [], [], [], [], []
    for l in range(DEPTH):
        mla_w = (mla_q_norm[l], mla_w_qb[l], mla_kv_norm[l], mla_w_uk[l])
        cmp_w = (nsa_cmp_pos[l], nsa_cmp_w1[l], nsa_cmp_b1[l], nsa_cmp_w2[l])
        xp = macaron_half(xp, ffn_w_gate[l, 0], ffn_w_up[l, 0], ffn_w_down[l, 0], ln_g[l, 0], ln_b[l, 0])
        xs = macaron_half(xs, ffn_w_gate[l, 0], ffn_w_up[l, 0], ffn_w_down[l, 0], ln_g[l, 0], ln_b[l, 0])
        moba, mla, nsa = project(xp, pos_p, w_in[l], *mla_w)
        o_a = moba_prompt(*moba, tab_a)
        o_b = mla_out(mla_prompt(*mla), mla_w_uv[l])
        o_c = nsa_prompt(*nsa, cmp_w, tab_c)
        xp = layer_norm(ALPHA * xp + jnp.concatenate([o_a, o_b, o_c], -1) @ w_out[l], ln_g[l, 1], ln_b[l, 1])
        moba_p.append(jnp.stack([moba[1], moba[2]], axis=2))
        mla_p.append(jnp.concatenate([mla[2], mla[3]], -1))
        cmp_p.append(nsa[1])
        sel_p.append(nsa[2])
        win_p.append(nsa[3][:, seq - min(WINDOW, seq):])
        moba, mla, nsa = project(xs, pos_s, w_in[l], *mla_w)
        o_a = moba_sample(*moba, cache_moba_kv, l, page_table, tab_a)
        o_b = mla_out(mla_sample(*mla, cache_mla, l, page_table), mla_w_uv[l])
        o_c, win_new = nsa_sample(*nsa, cache_nsa_cmp_kv, cache_nsa_sel_kv, state_nsa_win_kv, l, page_table, cmp_w, tab_c)
        xs = layer_norm(ALPHA * xs + jnp.concatenate([o_a, o_b, o_c], -1) @ w_out[l], ln_g[l, 1], ln_b[l, 1])
        moba_s.append(jnp.stack([moba[1], moba[2]], axis=2))
        mla_s.append(jnp.concatenate([mla[2], mla[3]], -1))
        cmp_s.append(nsa[1])
        sel_s.append(nsa[2])
        win_s.append(win_new)
        xp = macaron_half(xp, ffn_w_gate[l, 1], ffn_w_up[l, 1], ffn_w_down[l, 1], ln_g[l, 2], ln_b[l, 2])
        xs = macaron_half(xs, ffn_w_gate[l, 1], ffn_w_up[l, 1], ffn_w_down[l, 1], ln_g[l, 2], ln_b[l, 2])
    return (xp, xs, jnp.stack(moba_p), jnp.stack(moba_s), jnp.stack(mla_p), jnp.stack(mla_s), jnp.stack(cmp_p), jnp.stack(cmp_s),
            jnp.stack(sel_p), jnp.stack(sel_s), jnp.stack(win_p), jnp.stack(win_s))
```

```python
import functools
import math

import numpy as np
import jax
import jax.numpy as jnp
from jax import lax
from jax.experimental import pallas as pl
from jax.experimental.pallas import tpu as pltpu

f32 = jnp.float32
bf16 = jnp.bfloat16

D_MODEL = 1024
DEPTH = 4
PAGE_SIZE = 128
HEAD_DIM = 64
A_HEADS = 4
A_KV_HEADS = 2
MOBA_BLOCK = 256
MOBA_TOPK = 3
B_HEADS = 8
B_Q_RANK = 256
B_KV_RANK = 128
B_NOPE = 64
B_ROPE = 32
B_V = 64
ROPE_THETA = 10000.0
C_HEADS = 4
CMP_STRIDE = 16
CMP_BLOCK = 2 * CMP_STRIDE
CMP_HIDDEN = 128
SEL_BLOCK = 64
SEL_TOPK = 16
WINDOW = 512
NUM_BUCKETS = 32
MAX_DISTANCE = 128
D_FF = 2816
ALPHA = (2 * DEPTH) ** 0.25
FORCE_SCORE = 1e9
LN_EPS = 1e-5
RMS_EPS = 1e-6
NEG = -1e30

LANES = 128
VMEM_LIMIT = 56 * 1024 * 1024

IN_W = 1920


def _cparams(sem):
    return pltpu.CompilerParams(dimension_semantics=sem, vmem_limit_bytes=VMEM_LIMIT)


def _nt(a, b):
    return lax.dot_general(a, b, (((1,), (1,)), ((), ())), preferred_element_type=f32)


def _mm(a, b):
    return jnp.dot(a, b, preferred_element_type=f32)


def _sigmoid(x):
    return 1.0 / (1.0 + jnp.exp(-x))


def _layer_norm(y, g, b):
    mu = jnp.mean(y, axis=-1, keepdims=True)
    d = y - mu
    var = jnp.mean(d * d, axis=-1, keepdims=True)
    return d * lax.rsqrt(var + LN_EPS) * g + b


def _rms_norm(x, g):
    return x * lax.rsqrt(jnp.mean(x * x, axis=-1, keepdims=True) + RMS_EPS) * g


def _const_spec(shape):
    nd = len(shape)
    return pl.BlockSpec(shape, lambda *_: (0,) * nd, pipeline_mode=pl.Buffered(1))


def _ffn_kernel(x_ref, wg_ref, wu_ref, wd_ref, g_ref, b_ref, o_ref, *, tf):
    x = x_ref[...]
    xb = x.astype(bf16)
    acc = jnp.zeros(x.shape, f32)
    for c in range(D_FF // tf):
        sl = slice(c * tf, (c + 1) * tf)
        gt = _mm(xb, wg_ref[:, sl])
        up = _mm(xb, wu_ref[:, sl])
        h = (gt * _sigmoid(gt) * up).astype(bf16)
        acc = acc + _mm(h, wd_ref[sl, :])
    o_ref[...] = _layer_norm(ALPHA * x + 0.5 * acc, g_ref[...], b_ref[...])


def _ffn(x, wg, wu, wd, g, b, tm):
    n = x.shape[0]
    return pl.pallas_call(
        functools.partial(_ffn_kernel, tf=256),
        out_shape=jax.ShapeDtypeStruct((n, D_MODEL), f32),
        grid=(n // tm,),
        in_specs=[pl.BlockSpec((tm, D_MODEL), lambda i: (i, 0)),
                  _const_spec((D_MODEL, D_FF)), _const_spec((D_MODEL, D_FF)), _const_spec((D_FF, D_MODEL)),
                  _const_spec((1, D_MODEL)), _const_spec((1, D_MODEL))],
        out_specs=pl.BlockSpec((tm, D_MODEL), lambda i: (i, 0)),
        compiler_params=_cparams(("parallel",)),
        name="ffn_ln",
    )(x, wg, wu, wd, g, b)


def _outproj_kernel(x_ref, oa_ref, ob_ref, oc_ref, w_ref, g_ref, b_ref, o_ref):
    mix = (_mm(oa_ref[...].astype(bf16), w_ref[0:256, :])
           + _mm(ob_ref[...].astype(bf16), w_ref[256:768, :])
           + _mm(oc_ref[...].astype(bf16), w_ref[768:1024, :]))
    o_ref[...] = _layer_norm(ALPHA * x_ref[...] + mix, g_ref[...], b_ref[...])


def _outproj(x, oa, ob, oc, w, g, b, tm):
    n = x.shape[0]
    row = lambda c: pl.BlockSpec((tm, c), lambda i: (i, 0))
    return pl.pallas_call(
        _outproj_kernel,
        out_shape=jax.ShapeDtypeStruct((n, D_MODEL), f32),
        grid=(n // tm,),
        in_specs=[row(D_MODEL), row(256), row(512), row(256), _const_spec((1024, D_MODEL)),
                  _const_spec((1, D_MODEL)), _const_spec((1, D_MODEL))],
        out_specs=row(D_MODEL),
        compiler_params=_cparams(("parallel",)),
        name="outproj_ln",
    )(x, oa, ob, oc, w, g, b)


def _inproj_kernel(x_ref, w_ref, wn_ref, wr_ref, wrs_ref, wuk_ref, qg_ref, kg_ref, cos_ref, sin_ref,
                   aq_ref, mkv_ref, q_ref, kpad_ref, cq_ref, cmp_ref, sel_ref, win_ref, gate_ref,
                   selb_ref, winb_ref, mkvb_ref, kmean_ref, mkvt_ref, mlat_ref, cmpt_ref, selt_ref, wint_ref):
    y = _mm(x_ref[...].astype(bf16), w_ref[...])
    tm = y.shape[0]
    aq_ref[...] = y[:, 0:256].astype(bf16)
    mkv = y[:, 256:512]
    mkv_ref[...] = mkv
    mkvb_ref[...] = mkv.astype(bf16)
    mkvt_ref[0] = mkv.T
    for n in range(kmean_ref.shape[1]):
        kmean_ref[0, n:n + 1, :] = jnp.mean(mkv[n * MOBA_BLOCK:(n + 1) * MOBA_BLOCK, 0:128], axis=0, keepdims=True)
    cosv = cos_ref[...]
    sinv = sin_ref[...]
    qn = _rms_norm(y[:, 512:768], qg_ref[...]).astype(bf16)
    nope = _mm(qn, wn_ref[...])
    qpe = _mm(qn, wr_ref[...]) * cosv + _mm(qn, wrs_ref[...]) * sinv
    lane = lax.broadcasted_iota(jnp.int32, (tm, LANES), 1)
    for j in range(B_HEADS // 2):
        ql = _mm(nope[:, 128 * j:128 * (j + 1)].astype(bf16), wuk_ref[j])
        for t in range(2):
            h = 2 * j + t
            q_ref[h, :, 0:128] = ql[:, 128 * t:128 * (t + 1)].astype(bf16)
            src = qpe[:, 128 * (h // 4):128 * (h // 4 + 1)]
            shift = (LANES - B_ROPE * (h % 4)) % LANES
            if shift:
                src = pltpu.roll(src, shift, 1)
            q_ref[h, :, 128:256] = jnp.where(lane < B_ROPE, src, 0.0).astype(bf16)
    ckv = _rms_norm(y[:, 768:896], kg_ref[...])
    kpe = y[:, 896:1024] * cosv[:, 0:128] + y[:, 1024:1152] * sinv[:, 0:128]
    kpad_ref[:, 0:128] = ckv.astype(bf16)
    kpad_ref[:, 128:256] = kpe.astype(bf16)
    mlat_ref[0, 0:128, :] = ckv.T
    mlat_ref[0, 128:160, :] = kpe.T[0:B_ROPE, :]
    cq_ref[...] = y[:, 1152:1408].astype(bf16)
    cmp = y[:, 1408:1536]
    sel = y[:, 1536:1664]
    win = y[:, 1664:1792]
    cmp_ref[...] = cmp
    sel_ref[...] = sel
    win_ref[...] = win
    selb_ref[...] = sel.astype(bf16)
    winb_ref[...] = win.astype(bf16)
    cmpt_ref[0] = cmp.T
    selt_ref[0] = sel.T
    wint_ref[0] = win.T
    gate_ref[...] = _sigmoid(y[:, 1792:1920])


def _inproj(x, lw, cos_t, sin_t, tm, nb):
    n = x.shape[0]
    s = n // nb
    spb = s // tm
    row = lambda c: pl.BlockSpec((tm, c), lambda i: (i, 0))
    tab = pl.BlockSpec((tm, 256), lambda i: (i % spb, 0))
    fmaj = lambda f: pl.BlockSpec((1, f, tm), lambda i: (i // spb, 0, i % spb))
    nkm = max(tm // MOBA_BLOCK, 1)
    sd = jax.ShapeDtypeStruct
    outs = [sd((n, 256), bf16), sd((n, 256), f32), sd((B_HEADS, n, 256), bf16), sd((n, 256), bf16),
            sd((n, 256), bf16), sd((n, 128), f32), sd((n, 128), f32), sd((n, 128), f32),
            sd((n, 128), f32), sd((n, 128), bf16), sd((n, 128), bf16), sd((n, 256), bf16),
            sd((n // tm, nkm, 128), f32), sd((nb, 256, s), f32), sd((nb, 160, s), f32), sd((nb, 128, s), f32),
            sd((nb, 128, s), f32), sd((nb, 128, s), f32)]
    out_specs = [row(256), row(256), pl.BlockSpec((B_HEADS, tm, 256), lambda i: (0, i, 0)), row(256),
                 row(256), row(128), row(128), row(128), row(128), row(128), row(128), row(256),
                 pl.BlockSpec((1, nkm, 128), lambda i: (i, 0, 0)), fmaj(256), fmaj(160), fmaj(128), fmaj(128), fmaj(128)]
    names = ("aq", "mkv", "q", "kpad", "cq", "cmp", "sel", "win", "gate", "selb", "winb", "mkvb", "kmean",
             "mkv_t", "mla_t", "cmp_t", "sel_t", "win_t")
    res = pl.pallas_call(
        _inproj_kernel,
        out_shape=outs,
        grid=(n // tm,),
        in_specs=[row(D_MODEL), _const_spec((D_MODEL, IN_W)), _const_spec((256, 512)), _const_spec((256, 256)),
                  _const_spec((256, 256)), _const_spec((4, 128, 256)), _const_spec((1, 256)), _const_spec((1, 128)),
                  tab, tab],
        out_specs=out_specs,
        compiler_params=_cparams(("parallel",)),
        name="in_proj",
    )(x, lw["w_in"], lw["wn"], lw["wr"], lw["wrs"], lw["wuk"], lw["qg"], lw["kg"], cos_t, sin_t)
    return dict(zip(names, res))


def _online_update(m_sc, l_sc, acc_sc, s, valid, v):
    m_prev = m_sc[...]
    m_new = jnp.maximum(m_prev, jnp.max(s, axis=1, keepdims=True))
    alpha = jnp.exp(m_prev - m_new)
    p = jnp.exp(s - m_new)
    if valid is not None:
        p = jnp.where(valid, p, 0.0)
    l_sc[...] = alpha * l_sc[...] + jnp.sum(p, axis=1, keepdims=True)
    acc_sc[...] = alpha * acc_sc[...] + _mm(p.astype(bf16), v)
    m_sc[...] = m_new


def _init_softmax(m_sc, l_sc, acc_sc):
    m_sc[...] = jnp.full(m_sc.shape, NEG, f32)
    l_sc[...] = jnp.zeros(l_sc.shape, f32)
    acc_sc[...] = jnp.zeros(acc_sc.shape, f32)


def _finish(l_sc, acc_sc):
    return acc_sc[...] * (1.0 / jnp.maximum(l_sc[...], 1e-30))


def _stack_heads_lo(qt, lo):
    t = qt.shape[0]
    lane = lax.broadcasted_iota(jnp.int32, (t, LANES), 1)
    a, b = qt[:, 0:128], qt[:, 128:256]
    low = lane < HEAD_DIM
    h0 = jnp.where(low, a, 0.0)
    h1 = jnp.where(low, pltpu.roll(a, HEAD_DIM, 1), 0.0)
    if lo:
        h2 = jnp.where(low, b, 0.0)
        h3 = jnp.where(low, pltpu.roll(b, HEAD_DIM, 1), 0.0)
    else:
        h2 = jnp.where(low, 0.0, pltpu.roll(b, HEAD_DIM, 1))
        h3 = jnp.where(low, 0.0, b)
    return jnp.concatenate([h0, h1, h2, h3], axis=0)


def _unstack_heads(o, hi_all):
    t = o.shape[0] // 4
    lane = lax.broadcasted_iota(jnp.int32, (t, LANES), 1)
    low = lane < HEAD_DIM
    o0, o1, o2, o3 = (o[i * t:(i + 1) * t] for i in range(4))
    if hi_all:
        left = jnp.where(low, pltpu.roll(o0, HEAD_DIM, 1), o1)
    else:
        left = jnp.where(low, o0, pltpu.roll(o1, HEAD_DIM, 1))
    right = jnp.where(low, pltpu.roll(o2, HEAD_DIM, 1), o3)
    return jnp.concatenate([left, right], axis=1)


def _topk_mask(vals, k, lanef):
    sel = jnp.zeros(vals.shape, f32)
    big = jnp.float32(1e9)
    for _ in range(k):
        mx = jnp.max(vals, axis=1, keepdims=True)
        idx = jnp.min(jnp.where(vals == mx, lanef, big), axis=1, keepdims=True)
        hit = lanef == idx
        sel = jnp.where(hit & (mx > -jnp.inf), 1.0, sel)
        vals = jnp.where(hit, -jnp.inf, vals)
    return sel


def _mla_prompt_kernel(q_ref, k_ref, wuv_ref, o_ref, m_sc, l_sc, acc_sc, *, tq, tk, scale):
    qi = pl.program_id(1)
    q = q_ref[...].reshape(B_HEADS * tq, 256)
    _init_softmax(m_sc, l_sc, acc_sc)
    nfull = (qi * tq) // tk

    def full_step(kt, c):
        k = k_ref[pl.ds(pl.multiple_of(kt * tk, tk), tk), :]
        _online_update(m_sc, l_sc, acc_sc, _nt(q, k) * scale, None, k[:, 0:128])
        return c

    lax.fori_loop(0, nfull, full_step, 0)
    k = k_ref[pl.ds(pl.multiple_of(nfull * tk, tk), tk), :]
    s = _nt(q, k) * scale
    rows = lax.broadcasted_iota(jnp.int32, s.shape, 0)
    cols = lax.broadcasted_iota(jnp.int32, s.shape, 1)
    valid = (nfull * tk + cols) <= (qi * tq + (rows & (tq - 1)))
    _online_update(m_sc, l_sc, acc_sc, jnp.where(valid, s, NEG), valid, k[:, 0:128])
    o = _finish(l_sc, acc_sc).astype(bf16)
    out = _mm(o[0:tq], wuv_ref[0])
    for h in range(1, B_HEADS):
        out = out + _mm(o[h * tq:(h + 1) * tq], wuv_ref[h])
    o_ref[...] = out


def _mla_prompt(q, kpad, wuv, nb, s, tq=128, tk=512):
    nq = s // tq
    r = B_HEADS * tq
    return pl.pallas_call(
        functools.partial(_mla_prompt_kernel, tq=tq, tk=tk, scale=(B_NOPE + B_ROPE) ** -0.5),
        out_shape=jax.ShapeDtypeStruct((nb * s, 512), f32),
        grid=(nb, nq),
        in_specs=[pl.BlockSpec((B_HEADS, tq, 256), lambda b, i: (0, b * nq + i, 0)),
                  pl.BlockSpec((s, 256), lambda b, i: (b, 0), pipeline_mode=pl.Buffered(1)),
                  _const_spec((B_HEADS, 128, 512))],
        out_specs=pl.BlockSpec((tq, 512), lambda b, i: (b * nq + i, 0)),
        scratch_shapes=[pltpu.VMEM((r, 1), f32), pltpu.VMEM((r, 1), f32), pltpu.VMEM((r, 128), f32)],
        compiler_params=_cparams(("parallel", "arbitrary")),
        name="mla_prompt",
    )(q, kpad, wuv)


def _moba_prompt_kernel(aq_ref, kv_ref, kmean_ref, bias_ref, o_ref, m_sc, l_sc, acc_sc, *, tq, nblk, scale):
    qi = pl.program_id(1)
    blk = MOBA_BLOCK
    qpad = _stack_heads_lo(aq_ref[...].astype(f32), lo=False).astype(bf16)
    r = 4 * tq
    own = (qi * tq) // blk
    lanef = lax.broadcasted_iota(jnp.int32, (r, nblk), 1).astype(f32)
    ownf = own.astype(f32)
    gate = _nt(qpad, kmean_ref[0].astype(bf16))
    gate = jnp.where(lanef < ownf, gate, -jnp.inf)
    sel = _topk_mask(gate, min(MOBA_TOPK, nblk), lanef)
    _init_softmax(m_sc, l_sc, acc_sc)

    def past(n, c):
        kt = kv_ref[pl.ds(pl.multiple_of(n * blk, blk), blk), :]
        var = jnp.minimum((qi * tq - n * blk) // 128, 3)
        s = _nt(qpad, kt[:, 0:128]) * scale + bias_ref[var]
        nf = n.astype(f32)
        picked = jnp.sum(jnp.where(lanef == nf, sel, 0.0), axis=1, keepdims=True) > 0.5
        _online_update(m_sc, l_sc, acc_sc, jnp.where(picked, s, NEG), picked, kt[:, 128:256])
        return c

    lax.fori_loop(0, own, past, 0)
    kt = kv_ref[pl.ds(pl.multiple_of(own * blk, blk), blk), :]
    var = (qi * tq - own * blk) // 128
    s = _nt(qpad, kt[:, 0:128]) * scale + bias_ref[var]
    rows = lax.broadcasted_iota(jnp.int32, s.shape, 0)
    cols = lax.broadcasted_iota(jnp.int32, s.shape, 1)
    valid = (own * blk + cols) <= (qi * tq + (rows & (tq - 1)))
    _online_update(m_sc, l_sc, acc_sc, jnp.where(valid, s, NEG), valid, kt[:, 128:256])
    o_ref[...] = _unstack_heads(_finish(l_sc, acc_sc), hi_all=False)


def _moba_prompt(aq, mkvb, kmean, bias, nb, s, tq=128):
    nq = s // tq
    nblk = s // MOBA_BLOCK
    assert s % MOBA_BLOCK == 0
    r = 4 * tq
    return pl.pallas_call(
        functools.partial(_moba_prompt_kernel, tq=tq, nblk=nblk, scale=HEAD_DIM ** -0.5),
        out_shape=jax.ShapeDtypeStruct((nb * s, 256), f32),
        grid=(nb, nq),
        in_specs=[pl.BlockSpec((tq, 256), lambda b, i: (b * nq + i, 0)),
                  pl.BlockSpec((s, 256), lambda b, i: (b, 0), pipeline_mode=pl.Buffered(1)),
                  pl.BlockSpec((1, nblk, 128), lambda b, i: (b, 0, 0)),
                  _const_spec((4, r, MOBA_BLOCK))],
        out_specs=pl.BlockSpec((tq, 256), lambda b, i: (b * nq + i, 0)),
        scratch_shapes=[pltpu.VMEM((r, 1), f32), pltpu.VMEM((r, 1), f32), pltpu.VMEM((r, 128), f32)],
        compiler_params=_cparams(("parallel", "arbitrary")),
        name="moba_prompt",
    )(aq, mkvb, kmean, bias)


def _compress_tail(ya, yb, b1, w2):
    n_sub = ya.shape[0]
    hid = ya + pltpu.roll(yb, n_sub - 1, 0) + b1
    act = (hid * _sigmoid(hid)).astype(bf16)
    ckv = _mm(act, w2)
    rows = lax.broadcasted_iota(jnp.int32, ckv.shape, 0)
    return jnp.where(rows < n_sub - 1, ckv, 0.0)


def _compress_kernel(x_ref, pea_ref, peb_ref, wa_ref, wb_ref, b1_ref, w2_ref, o_ref):
    x = x_ref[...]
    ya = _mm((x + pea_ref[...]).astype(bf16), wa_ref[...])
    yb = _mm((x + peb_ref[...]).astype(bf16), wb_ref[...])
    o_ref[...] = _compress_tail(ya, yb, b1_ref[...], w2_ref[...]).astype(bf16)


def _compress_prompt(xc, lw, nb, n_sub):
    return pl.pallas_call(
        _compress_kernel,
        out_shape=jax.ShapeDtypeStruct((nb * n_sub, 128), bf16),
        grid=(nb,),
        in_specs=[pl.BlockSpec((n_sub, 2048), lambda b: (b, 0)), _const_spec((1, 2048)), _const_spec((1, 2048)),
                  _const_spec((2048, 256)), _const_spec((2048, 256)), _const_spec((1, 256)), _const_spec((256, 128))],
        out_specs=pl.BlockSpec((n_sub, 128), lambda b: (b, 0)),
        compiler_params=_cparams(("parallel",)),
        name="nsa_compress",
    )(xc, lw["pea"], lw["peb"], lw["wa"], lw["wb"], lw["b1"], lw["w2"])


def _nsa_cmp_kernel(cq_ref, ckv_ref, pb_ref, tosel_ref, oc_ref, selm_ref, *, tq, n_sub, ns, scale):
    qi = pl.program_id(1)
    q0 = qi * tq
    qpad = _stack_heads_lo(cq_ref[...].astype(f32), lo=True).astype(bf16)
    ckv = ckv_ref[...]
    r = 4 * tq
    bias = pltpu.roll(pb_ref[...], q0 // CMP_STRIDE, 1)
    s = _nt(qpad, ckv) * scale + bias
    rows = lax.broadcasted_iota(jnp.int32, (r, n_sub), 0)
    cols = lax.broadcasted_iota(jnp.int32, (r, n_sub), 1)
    qpos = q0 + (rows & (tq - 1))
    valid = (cols * CMP_STRIDE + (CMP_BLOCK - 1)) <= qpos
    s = jnp.where(valid, s, NEG)
    m = jnp.max(s, axis=1, keepdims=True)
    e = jnp.where(valid, jnp.exp(s - m), 0.0)
    p = e * (1.0 / jnp.maximum(jnp.sum(e, axis=1, keepdims=True), 1e-30))
    pb = p.astype(bf16)
    oc_ref[...] = _unstack_heads(_mm(pb, ckv), hi_all=True)
    tosel = tosel_ref[...]
    imp = _mm(pb[0:tq], tosel)
    for h in range(1, C_HEADS):
        imp = imp + _mm(pb[h * tq:(h + 1) * tq], tosel)
    lanef = lax.broadcasted_iota(jnp.int32, imp.shape, 1).astype(f32)
    irow = lax.broadcasted_iota(jnp.int32, imp.shape, 0)
    cur = ((q0 + irow) // SEL_BLOCK).astype(f32)
    imp = jnp.where((lanef == 0.0) | (lanef == cur) | (lanef == cur - 1.0), FORCE_SCORE, imp)
    imp = jnp.where(lanef <= cur, imp, -jnp.inf)
    selm_ref[...] = _topk_mask(imp, min(SEL_TOPK, ns), lanef).astype(bf16)


def _nsa_cmp(cq, ckv, pbias, tosel, nb, s, tq=128):
    nq = s // tq
    n_sub = s // CMP_STRIDE
    ns = s // SEL_BLOCK
    nsp = tosel.shape[1]
    return pl.pallas_call(
        functools.partial(_nsa_cmp_kernel, tq=tq, n_sub=n_sub, ns=ns, scale=HEAD_DIM ** -0.5),
        out_shape=[jax.ShapeDtypeStruct((nb * s, 256), f32), jax.ShapeDtypeStruct((nb * s, nsp), bf16)],
        grid=(nb, nq),
        in_specs=[pl.BlockSpec((tq, 256), lambda b, i: (b * nq + i, 0)),
                  pl.BlockSpec((n_sub, 128), lambda b, i: (b, 0)),
                  _const_spec((4 * tq, n_sub)), _const_spec((n_sub, nsp))],
        out_specs=[pl.BlockSpec((tq, 256), lambda b, i: (b * nq + i, 0)),
                   pl.BlockSpec((tq, nsp), lambda b, i: (b * nq + i, 0))],
        compiler_params=_cparams(("parallel", "parallel")),
        name="nsa_cmp_select",
    )(cq, ckv, pbias, tosel)


def _nsa_main_kernel(cq_ref, sel_ref, win_ref, selm_ref, oc_ref, g_ref, bsel_ref, bwin_ref, o_ref,
                     m_sc, l_sc, acc_sc, *, tq, tk, wk, scale):
    qi = pl.program_id(1)
    q0 = qi * tq
    r = 4 * tq
    qpad = _stack_heads_lo(cq_ref[...].astype(f32), lo=True).astype(bf16)
    selm = selm_ref[...]
    nsp = selm.shape[1]
    bper = tk // SEL_BLOCK
    erow = lax.broadcasted_iota(jnp.int32, (nsp, tk), 0)
    ecol = lax.broadcasted_iota(jnp.int32, (nsp, tk), 1) // SEL_BLOCK
    nfull = q0 // tk
    _init_softmax(m_sc, l_sc, acc_sc)

    def picked_mask(kt):
        expand = (erow == kt * bper + ecol).astype(bf16)
        hit = _mm(selm, expand) > 0.5
        return jnp.concatenate([hit] * 4, axis=0)

    def full_step(kt, c):
        k = sel_ref[pl.ds(pl.multiple_of(kt * tk, tk), tk), :]
        var = jnp.minimum((q0 - kt * tk) // 128, tk // 128 + 1)
        s = _nt(qpad, k) * scale + bsel_ref[var]
        hit = picked_mask(kt)
        _online_update(m_sc, l_sc, acc_sc, jnp.where(hit, s, NEG), hit, k)
        return c

    lax.fori_loop(0, nfull, full_step, 0)
    k = sel_ref[pl.ds(pl.multiple_of(nfull * tk, tk), tk), :]
    s = _nt(qpad, k) * scale + bsel_ref[(q0 - nfull * tk) // 128]
    rows = lax.broadcasted_iota(jnp.int32, s.shape, 0)
    cols = lax.broadcasted_iota(jnp.int32, s.shape, 1)
    qpos = q0 + (rows & (tq - 1))
    valid = picked_mask(nfull) & ((nfull * tk + cols) <= qpos)
    _online_update(m_sc, l_sc, acc_sc, jnp.where(valid, s, NEG), valid, k)
    o_s = _finish(l_sc, acc_sc)
    ks = jnp.maximum(q0 - WINDOW, 0)
    kw = win_ref[pl.ds(pl.multiple_of(ks, 128), wk), :]
    sw = _nt(qpad, kw) * scale + bwin_ref[(q0 - ks) // 128]
    rows = lax.broadcasted_iota(jnp.int32, sw.shape, 0)
    cols = lax.broadcasted_iota(jnp.int32, sw.shape, 1)
    dw = q0 + (rows & (tq - 1)) - (ks + cols)
    vw = (dw >= 0) & (dw <= WINDOW)
    sw = jnp.where(vw, sw, NEG)
    mw = jnp.max(sw, axis=1, keepdims=True)
    ew = jnp.where(vw, jnp.exp(sw - mw), 0.0)
    lw = jnp.maximum(jnp.sum(ew, axis=1, keepdims=True), 1e-30)
    o_w = _mm(ew.astype(bf16), kw) * (1.0 / lw)
    g = g_ref[...]
    os4 = _unstack_heads(o_s, hi_all=True)
    ow4 = _unstack_heads(o_w, hi_all=True)
    oc4 = oc_ref[...]
    lane = lax.broadcasted_iota(jnp.int32, (tq, 256), 1) // HEAD_DIM
    gc = jnp.zeros((tq, 256), f32)
    gs = jnp.zeros((tq, 256), f32)
    gw = jnp.zeros((tq, 256), f32)
    for h in range(C_HEADS):
        gc = jnp.where(lane == h, g[:, 3 * h:3 * h + 1], gc)
        gs = jnp.where(lane == h, g[:, 3 * h + 1:3 * h + 2], gs)
        gw = jnp.where(lane == h, g[:, 3 * h + 2:3 * h + 3], gw)
    o_ref[...] = gc * oc4 + gs * os4 + gw * ow4


def _nsa_main(cq, selb, winb, selm, oc, gate, bsel, bwin, nb, s, tq=128, tk=512):
    nq = s // tq
    nsp = selm.shape[1]
    wk = WINDOW + tq
    r = 4 * tq
    qrow = lambda c: pl.BlockSpec((tq, c), lambda b, i: (b * nq + i, 0))
    seq = lambda c: pl.BlockSpec((s, c), lambda b, i: (b, 0), pipeline_mode=pl.Buffered(1))
    return pl.pallas_call(
        functools.partial(_nsa_main_kernel, tq=tq, tk=tk, wk=wk, scale=HEAD_DIM ** -0.5),
        out_shape=jax.ShapeDtypeStruct((nb * s, 256), f32),
        grid=(nb, nq),
        in_specs=[qrow(256), seq(128), seq(128), qrow(nsp), qrow(256), qrow(128),
                  _const_spec(bsel.shape), _const_spec(bwin.shape)],
        out_specs=qrow(256),
        scratch_shapes=[pltpu.VMEM((r, 1), f32), pltpu.VMEM((r, 1), f32), pltpu.VMEM((r, 128), f32)],
        compiler_params=_cparams(("parallel", "arbitrary")),
        name="nsa_sel_win",
    )(cq, selb, winb, selm, oc, gate, bsel, bwin)


def _bucket(dist):
    n = jnp.maximum(dist, 0)
    max_exact = NUM_BUCKETS // 2
    nf = jnp.maximum(n, 1).astype(f32)
    large = max_exact + (jnp.log(nf / max_exact) / math.log(MAX_DISTANCE / max_exact)
                         * (NUM_BUCKETS - max_exact)).astype(jnp.int32)
    return jnp.where(n < max_exact, n, jnp.minimum(large, NUM_BUCKETS - 1))


def _bias_by_dist(tab, dmax):
    return tab[:, _bucket(jnp.arange(dmax + 1, dtype=jnp.int32))]


def _toeplitz(bd, deltas, tq, tk, far):
    h, d1 = bd.shape
    i = np.arange(tq)[:, None]
    j = np.arange(tk)[None, :]
    tiles = [bd[:, np.clip(d + i - j, 0, d1 - 1)].reshape(h * tq, tk) for d in deltas]
    if far:
        tiles.append(jnp.broadcast_to(bd[:, d1 - 1][:, None, None], (h, tq, tk)).reshape(h * tq, tk))
    return jnp.stack(tiles)


def _cmp_bias_table(bd, tq, n_sub):
    h, d1 = bd.shape
    c = np.arange(n_sub)
    m = np.where(c < 8, c, c - n_sub)
    dist = np.arange(tq)[:, None] - CMP_STRIDE * m[None, :] - (CMP_BLOCK - 1)
    return bd[:, np.clip(dist, 0, d1 - 1)].reshape(h * tq, n_sub)


def _to_sel(n_sub, ns):
    nsp = -(-ns // LANES) * LANES
    i = np.arange(n_sub)[:, None] * CMP_STRIDE
    j = np.arange(nsp)[None, :] * SEL_BLOCK
    t = (i < j + SEL_BLOCK) & (i + CMP_BLOCK > j) & (np.arange(nsp)[None, :] < ns) & (np.arange(n_sub)[:, None] < n_sub - 1)
    return jnp.asarray(t, dtype=bf16)


def _rope_tables(pos):
    half = B_ROPE // 2
    inv = jnp.power(ROPE_THETA, -jnp.arange(half, dtype=f32) / half)
    ang = pos.astype(f32)[:, None] * inv
    c, s = jnp.cos(ang), jnp.sin(ang)
    return (jnp.tile(jnp.concatenate([c, c], 1), (1, B_HEADS)), jnp.tile(jnp.concatenate([-s, s], 1), (1, B_HEADS)))


def _prep_layer(l, w_in, w_out, mla_q_norm, mla_w_qb, mla_kv_norm, mla_w_uk, mla_w_uv,
                nsa_cmp_pos, nsa_cmp_w1, nsa_cmp_b1, nsa_cmp_w2):
    w = w_in[l]
    z = lambda n: jnp.zeros((D_MODEL, n), f32)
    w_in_p = jnp.concatenate([w[:, 0:928], z(96), w[:, 912:928], w[:, 896:912], z(96), w[:, 928:1568],
                              w[:, 1568:1580], z(116)], axis=1)
    qb = mla_w_qb[l]
    ukt = jnp.transpose(mla_w_uk[l], (1, 2, 0))
    z4 = jnp.zeros((4, 64, 128), f32)
    wuk = jnp.concatenate([jnp.concatenate([ukt[0::2], z4], axis=2), jnp.concatenate([z4, ukt[1::2]], axis=2)], axis=1)
    wuv = jnp.einsum('rhd,hg->hrgd', mla_w_uv[l], jnp.eye(B_HEADS, dtype=f32)).reshape(B_HEADS, 128, 512)
    pe = nsa_cmp_pos[l].reshape(2, 2, CMP_STRIDE, HEAD_DIM)
    w1r = nsa_cmp_w1[l].reshape(2, 2, CMP_STRIDE, HEAD_DIM, CMP_HIDDEN)
    eye2 = jnp.eye(2, dtype=f32)
    expand = lambda half: jnp.einsum('ksdj,kc->skdcj', w1r[:, half], eye2).reshape(2048, 256)
    return {
        "w_in": w_in_p.astype(bf16),
        "w_out": w_out[l].astype(bf16),
        "wn": qb[:, :, :B_NOPE].reshape(256, 512).astype(bf16),
        "wr": qb[:, :, B_NOPE:].reshape(256, 256).astype(bf16),
        "wrs": jnp.concatenate([qb[:, :, 80:96], qb[:, :, 64:80]], axis=-1).reshape(256, 256).astype(bf16),
        "wuk": wuk.astype(bf16),
        "wuv": wuv.astype(bf16),
        "qg": mla_q_norm[l].reshape(1, 256),
        "kg": mla_kv_norm[l].reshape(1, 128),
        "pea": jnp.transpose(pe[:, 0], (1, 0, 2)).reshape(1, 2048),
        "peb": jnp.transpose(pe[:, 1], (1, 0, 2)).reshape(1, 2048),
        "wa": expand(0).astype(bf16),
        "wb": expand(1).astype(bf16),
        "b1": nsa_cmp_b1[l].reshape(1, 256),
        "w2": jnp.einsum('kjd,kc->kjcd', nsa_cmp_w2[l], eye2).reshape(256, 128).astype(bf16),
    }


def _prompt_tables(rel_bias, s, tq=128):
    tab_a = rel_bias[:, :A_HEADS].T
    tab_c = rel_bias[:, A_HEADS:].T
    bd_a = _bias_by_dist(tab_a, 1024)
    bd_c = _bias_by_dist(tab_c, 2048)
    n_sub = s // CMP_STRIDE
    return {
        "moba": _toeplitz(bd_a, (0, 128, 256), tq, MOBA_BLOCK, far=True),
        "bsel": _toeplitz(bd_c, (0, 128, 256, 384, 512), tq, 512, far=True),
        "bwin": _toeplitz(bd_c, (0, 128, 256, 384, 512), tq, WINDOW + tq, far=False),
        "pcmp": _cmp_bias_table(bd_c, tq, n_sub),
        "tosel": _to_sel(n_sub, s // SEL_BLOCK),
    }


def _prompt_mixers(xp, lw, tabs, cos_t, sin_t, nb, s):
    pr = _inproj(xp, lw, cos_t, sin_t, tm=512, nb=nb)
    o_a = _moba_prompt(pr["aq"], pr["mkvb"], pr["kmean"].reshape(nb, s // MOBA_BLOCK, 128), tabs["moba"], nb, s)
    o_b = _mla_prompt(pr["q"], pr["kpad"], lw["wuv"], nb, s)
    ckv = _compress_prompt(pr["cmp"].reshape(nb * s // CMP_STRIDE, 2048), lw, nb, s // CMP_STRIDE)
    oc, selm = _nsa_cmp(pr["cq"], ckv, tabs["pcmp"], tabs["tosel"], nb, s)
    o_c = _nsa_main(pr["cq"], pr["selb"], pr["winb"], selm, oc, pr["gate"], tabs["bsel"], tabs["bwin"], nb, s)
    return pr, o_a, o_b, o_c


def _page_copy(pt_ref, cache, layer, buf, sem, seq, slot, p, n_pages):
    page = pt_ref[seq * n_pages + p]
    return pltpu.make_async_copy(cache.at[layer, page], buf.at[slot, :, pl.ds(p * PAGE_SIZE, PAGE_SIZE)],
                                 sem.at[slot])


def _prefetch_pages(pt_ref, cache, layer, buf, sem, n_pages, nseq):
    b = pl.program_id(0)
    slot = b % 2

    @pl.when(b == 0)
    def _():
        for p in range(n_pages):
            _page_copy(pt_ref, cache, layer, buf, sem, 0, 0, p, n_pages).start()

    @pl.when(b + 1 < nseq)
    def _():
        for p in range(n_pages):
            _page_copy(pt_ref, cache, layer, buf, sem, b + 1, 1 - slot, p, n_pages).start()

    for p in range(n_pages):
        _page_copy(pt_ref, cache, layer, buf, sem, b, slot, p, n_pages).wait()
    return slot


def _softmax_chunk(m, l, acc, s, hit, vt):
    m_new = jnp.maximum(m, jnp.max(s, axis=1, keepdims=True))
    alpha = jnp.exp(m - m_new)
    p = jnp.exp(s - m_new)
    if hit is not None:
        p = jnp.where(hit, p, 0.0)
    return m_new, alpha * l + jnp.sum(p, axis=1, keepdims=True), alpha * acc + _nt(p.astype(bf16), vt)


def _bf16_round(x):
    return x.astype(bf16).astype(f32)


def _mla_decode_kernel(pt_ref, q_ref, kn_ref, cache, o_ref, buf, sem, *, layer, n_pages, nseq, chunk, scale):
    slot = _prefetch_pages(pt_ref, cache, layer, buf, sem, n_pages, nseq)
    q = q_ref[0]
    knf = kn_ref[0].astype(f32)
    m = jnp.sum(q.astype(f32) * knf, axis=1, keepdims=True) * scale
    l = jnp.ones((B_HEADS, 1), f32)
    acc = jnp.broadcast_to(knf[:, 0:128], (B_HEADS, 128))
    q160 = q[:, 0:B_KV_RANK + B_ROPE]
    for c in range(n_pages * PAGE_SIZE // chunk):
        kc = buf[slot, :, c * chunk:(c + 1) * chunk].astype(bf16)
        m, l, acc = _softmax_chunk(m, l, acc, _mm(q160, kc) * scale, None, kc[0:128, :])
    o_ref[0] = acc * (1.0 / l)


def _mla_decode(pt, q, kn, cache, layer, chunk=1024):
    nseq = q.shape[0]
    n_pages = pt.shape[0] // nseq
    width = cache.shape[2]
    return pl.pallas_call(
        functools.partial(_mla_decode_kernel, layer=layer, n_pages=n_pages, nseq=nseq, chunk=chunk,
                          scale=(B_NOPE + B_ROPE) ** -0.5),
        out_shape=jax.ShapeDtypeStruct((nseq, B_HEADS, 128), f32),
        grid_spec=pltpu.PrefetchScalarGridSpec(
            num_scalar_prefetch=1, grid=(nseq,),
            in_specs=[pl.BlockSpec((1, B_HEADS, 256), lambda b, pt: (b, 0, 0)),
                      pl.BlockSpec((1, 1, 256), lambda b, pt: (b, 0, 0)),
                      pl.BlockSpec(memory_space=pl.ANY)],
            out_specs=pl.BlockSpec((1, B_HEADS, 128), lambda b, pt: (b, 0, 0)),
            scratch_shapes=[pltpu.VMEM((2, width, n_pages * PAGE_SIZE), f32), pltpu.SemaphoreType.DMA((2,))]),
        compiler_params=_cparams(("arbitrary",)),
        name="mla_decode",
    )(pt, q, kn, cache)


def _decode_qrows(qt, lo):
    return jnp.concatenate([_stack_heads_lo(qt, lo), jnp.zeros((4, LANES), f32)], axis=0)


def _moba_decode_kernel(pt_ref, aq_ref, kvn_ref, brow_ref, b0_ref, cache, o_ref, buf, sem,
                        *, layer, n_pages, nseq, chunk, scale):
    slot = _prefetch_pages(pt_ref, cache, layer, buf, sem, n_pages, nseq)
    past = n_pages * PAGE_SIZE
    nblk = past // MOBA_BLOCK
    qb = _decode_qrows(aq_ref[0].astype(f32), lo=False).astype(bf16)
    qf = qb.astype(f32)
    lane_k = lax.broadcasted_iota(jnp.int32, (128, LANES), 1)
    kmean_t = jnp.zeros((128, LANES), f32)
    for n in range(nblk):
        col = jnp.sum(buf[slot, 0:128, n * MOBA_BLOCK:(n + 1) * MOBA_BLOCK], axis=1, keepdims=True) * (1.0 / MOBA_BLOCK)
        kmean_t = jnp.where(lane_k == n, col, kmean_t)
    lanef = lax.broadcasted_iota(jnp.int32, (8, LANES), 1).astype(f32)
    gate = jnp.where(lanef < nblk, _mm(qb, kmean_t.astype(bf16)), -jnp.inf)
    selb = _topk_mask(gate, min(MOBA_TOPK, nblk), lanef).astype(bf16)
    kvn = kvn_ref[0]
    m = jnp.sum(qf * _bf16_round(kvn[:, 0:128]), axis=1, keepdims=True) * scale + b0_ref[...]
    l = jnp.ones((8, 1), f32)
    acc = jnp.broadcast_to(_bf16_round(kvn[:, 128:256]), (8, 128))
    bper = chunk // MOBA_BLOCK
    erow = lax.broadcasted_iota(jnp.int32, (LANES, chunk), 0)
    ecol = lax.broadcasted_iota(jnp.int32, (LANES, chunk), 1) // MOBA_BLOCK
    for c in range(past // chunk):
        kt = buf[slot, :, c * chunk:(c + 1) * chunk]
        s = _mm(qb, kt[0:128, :].astype(bf16)) * scale + brow_ref[:, c * chunk:(c + 1) * chunk]
        hit = _mm(selb, (erow == c * bper + ecol).astype(bf16)) > 0.5
        m, l, acc = _softmax_chunk(m, l, acc, jnp.where(hit, s, NEG), hit, kt[128:256, :].astype(bf16))
    o_ref[0] = acc * (1.0 / l)


def _moba_decode(pt, aq, kvn, brow, b0, cache, layer, chunk=1024):
    nseq = aq.shape[0]
    n_pages = pt.shape[0] // nseq
    past = n_pages * PAGE_SIZE
    assert past % MOBA_BLOCK == 0 and past // MOBA_BLOCK <= LANES
    return pl.pallas_call(
        functools.partial(_moba_decode_kernel, layer=layer, n_pages=n_pages, nseq=nseq, chunk=chunk,
                          scale=HEAD_DIM ** -0.5),
        out_shape=jax.ShapeDtypeStruct((nseq, 8, 128), f32),
        grid_spec=pltpu.PrefetchScalarGridSpec(
            num_scalar_prefetch=1, grid=(nseq,),
            in_specs=[pl.BlockSpec((1, 1, 256), lambda b, pt: (b, 0, 0)),
                      pl.BlockSpec((1, 1, 256), lambda b, pt: (b, 0, 0)),
                      pl.BlockSpec((8, past), lambda b, pt: (0, 0)),
                      pl.BlockSpec((8, 1), lambda b, pt: (0, 0)),
                      pl.BlockSpec(memory_space=pl.ANY)],
            out_specs=pl.BlockSpec((1, 8, 128), lambda b, pt: (b, 0, 0)),
            scratch_shapes=[pltpu.VMEM((2, 256, past), f32), pltpu.SemaphoreType.DMA((2,))]),
        compiler_params=_cparams(("arbitrary",)),
        name="moba_decode",
    )(pt, aq, kvn, brow, b0, cache)


def _nsa_decode_kernel(pt_ref, cq_ref, seln_ref, winn_ref, wint_ref, g_ref, st_ref, pea_ref, peb_ref, wa_ref,
                       wb_ref, b1_ref, w2_ref, bc_ref, bs_ref, bw_ref, b0_ref, tosel_ref, cmp_cache, sel_cache,
                       o_ref, wout_ref, xbuf, sbuf, tok, semx, sems, *, layer, n_pages, nseq, chunk, scale):
    slot = _prefetch_pages(pt_ref, cmp_cache, layer, xbuf, semx, n_pages, nseq)
    _prefetch_pages(pt_ref, sel_cache, layer, sbuf, sems, n_pages, nseq)
    past = n_pages * PAGE_SIZE
    n_sub = past // CMP_STRIDE
    ns = past // SEL_BLOCK
    qb = _decode_qrows(cq_ref[0].astype(f32), lo=True).astype(bf16)
    qf = qb.astype(f32)
    b0 = b0_ref[...]
    for c in range(past // chunk):
        tok[c * chunk:(c + 1) * chunk, :] = xbuf[slot, :, c * chunk:(c + 1) * chunk].T
    ya = jnp.zeros((n_sub, 2 * CMP_HIDDEN), f32)
    yb = jnp.zeros((n_sub, 2 * CMP_HIDDEN), f32)
    for t in range(CMP_STRIDE):
        rows_t = tok[pl.ds(t, n_sub, stride=CMP_STRIDE), :]
        ya = ya + _mm((rows_t + pea_ref[t:t + 1, :]).astype(bf16), wa_ref[t])
        yb = yb + _mm((rows_t + peb_ref[t:t + 1, :]).astype(bf16), wb_ref[t])
    ckv = _compress_tail(ya, yb, b1_ref[...], w2_ref[...]).astype(bf16)
    s = _nt(qb, ckv) * scale + bc_ref[...]
    cols = lax.broadcasted_iota(jnp.int32, s.shape, 1)
    valid = (cols * CMP_STRIDE + (CMP_BLOCK - 1)) <= past
    s = jnp.where(valid, s, NEG)
    e = jnp.where(valid, jnp.exp(s - jnp.max(s, axis=1, keepdims=True)), 0.0)
    pb = (e * (1.0 / jnp.maximum(jnp.sum(e, axis=1, keepdims=True), 1e-30))).astype(bf16)
    o_c = _mm(pb, ckv)
    imp = jnp.sum(_mm(pb, tosel_ref[...])[0:C_HEADS], axis=0, keepdims=True)
    imp = jnp.broadcast_to(imp, (8, imp.shape[1]))
    lanef = lax.broadcasted_iota(jnp.int32, imp.shape, 1).astype(f32)
    imp = jnp.where((lanef == 0.0) | (lanef == ns - 1.0), FORCE_SCORE, imp)
    imp = jnp.where(lanef < ns, imp, -jnp.inf)
    selb = _topk_mask(imp, min(SEL_TOPK - 1, ns), lanef).astype(bf16)
    kn = _bf16_round(seln_ref[0])
    m = jnp.sum(qf * kn, axis=1, keepdims=True) * scale + b0
    l = jnp.ones((8, 1), f32)
    acc = jnp.broadcast_to(kn, (8, 128))
    nsp = selb.shape[1]
    bper = chunk // SEL_BLOCK
    erow = lax.broadcasted_iota(jnp.int32, (nsp, chunk), 0)
    ecol = lax.broadcasted_iota(jnp.int32, (nsp, chunk), 1) // SEL_BLOCK
    for c in range(past // chunk):
        kt = sbuf[slot, :, c * chunk:(c + 1) * chunk].astype(bf16)
        sc = _mm(qb, kt) * scale + bs_ref[:, c * chunk:(c + 1) * chunk]
        hit = _mm(selb, (erow == c * bper + ecol).astype(bf16)) > 0.5
        m, l, acc = _softmax_chunk(m, l, acc, jnp.where(hit, sc, NEG), hit, kt)
    o_s = acc * (1.0 / l)
    st = st_ref[0]
    wb = st.shape[1]
    stb = st.astype(bf16)
    wnr = _bf16_round(winn_ref[0])
    sw = _mm(qb, stb) * scale + bw_ref[...]
    sn = jnp.sum(qf * wnr, axis=1, keepdims=True) * scale + b0
    mw = jnp.maximum(jnp.max(sw, axis=1, keepdims=True), sn)
    ew = jnp.exp(sw - mw)
    en = jnp.exp(sn - mw)
    o_w = (_nt(ew.astype(bf16), stb) + en * wnr) * (1.0 / (jnp.sum(ew, axis=1, keepdims=True) + en))
    g = g_ref[0]
    o_ref[0] = g[:, 0:1] * o_c + g[:, 1:2] * o_s + g[:, 2:3] * o_w
    seq_lane = lax.broadcasted_iota(jnp.int32, wint_ref.shape, 1)
    wcol = jnp.sum(jnp.where(seq_lane == pl.program_id(0), wint_ref[...], 0.0), axis=1, keepdims=True)
    lanes = lax.broadcasted_iota(jnp.int32, st.shape, 1)
    wout_ref[0] = jnp.where(lanes == wb - 1, wcol, pltpu.roll(st, wb - 1, 1))


def _nsa_decode(pt, cq, seln, winn, wint, g, state, lw, bc, bs, bw, b0, tosel, cmp_cache, sel_cache, layer,
                chunk=1024):
    nseq = cq.shape[0]
    n_pages = pt.shape[0] // nseq
    past = n_pages * PAGE_SIZE
    n_sub = past // CMP_STRIDE
    wbuf = state.shape[3]
    nsp = tosel.shape[1]
    assert past % CMP_STRIDE == 0 and wbuf <= WINDOW
    seq3 = lambda r, c: pl.BlockSpec((1, r, c), lambda b, pt: (b, 0, 0))
    cst = lambda shape: pl.BlockSpec(shape, lambda b, pt: (0,) * len(shape), pipeline_mode=pl.Buffered(1))
    return pl.pallas_call(
        functools.partial(_nsa_decode_kernel, layer=layer, n_pages=n_pages, nseq=nseq, chunk=chunk,
                          scale=HEAD_DIM ** -0.5),
        out_shape=[jax.ShapeDtypeStruct((nseq, 8, 128), f32), jax.ShapeDtypeStruct((nseq, 128, wbuf), f32)],
        grid_spec=pltpu.PrefetchScalarGridSpec(
            num_scalar_prefetch=1, grid=(nseq,),
            in_specs=[seq3(1, 256), seq3(1, 128), seq3(1, 128), cst((128, nseq)), seq3(8, 128),
                      pl.BlockSpec((None, 1, 128, wbuf), lambda b, pt: (layer, b, 0, 0)),
                      cst((CMP_STRIDE, 128)), cst((CMP_STRIDE, 128)), cst((CMP_STRIDE, 128, 256)),
                      cst((CMP_STRIDE, 128, 256)), cst((1, 256)),
                      cst((256, 128)), cst((8, n_sub)), cst((8, past)), cst((8, wbuf)), cst((8, 1)),
                      cst((n_sub, nsp)),
                      pl.BlockSpec(memory_space=pl.ANY), pl.BlockSpec(memory_space=pl.ANY)],
            out_specs=[seq3(8, 128), seq3(128, wbuf)],
            scratch_shapes=[pltpu.VMEM((2, 128, past), f32), pltpu.VMEM((2, 128, past), f32),
                            pltpu.VMEM((past, 128), f32),
                            pltpu.SemaphoreType.DMA((2,)), pltpu.SemaphoreType.DMA((2,))]),
        compiler_params=_cparams(("arbitrary",)),
        name="nsa_decode",
    )(pt, cq, seln, winn, wint, g, state, lw["pea"].reshape(CMP_STRIDE, 128), lw["peb"].reshape(CMP_STRIDE, 128),
      lw["wa"].reshape(CMP_STRIDE, 128, 256), lw["wb"].reshape(CMP_STRIDE, 128, 256), lw["b1"], lw["w2"],
      bc, bs, bw, b0, tosel, cmp_cache, sel_cache)


def _dense_kernel(a_ref, w_ref, o_ref):
    o_ref[...] = _mm(a_ref[...].astype(bf16), w_ref[...])


def _dense(a, w):
    return pl.pallas_call(
        _dense_kernel,
        out_shape=jax.ShapeDtypeStruct((a.shape[0], w.shape[1]), f32),
        name="dense",
    )(a, w)


def _sample_tables(rel_bias, past, wbuf):
    tab_a = rel_bias[:, :A_HEADS].T
    tab_c = rel_bias[:, A_HEADS:].T
    pad = lambda t: jnp.concatenate([t, jnp.zeros_like(t)], axis=0)
    kpos = np.arange(past)
    n_sub = past // CMP_STRIDE
    c_end = np.arange(n_sub) * CMP_STRIDE + CMP_BLOCK - 1
    return {
        "brow_a": pad(tab_a[:, _bucket(jnp.asarray(past - kpos, jnp.int32))]),
        "b0_a": pad(tab_a[:, _bucket(jnp.zeros((1,), jnp.int32))]),
        "bc": pad(tab_c[:, _bucket(jnp.asarray(past - c_end, jnp.int32))]),
        "bs": pad(tab_c[:, _bucket(jnp.asarray(past - kpos, jnp.int32))]),
        "bw": pad(tab_c[:, _bucket(jnp.asarray(wbuf - np.arange(wbuf), jnp.int32))]),
        "b0_c": pad(tab_c[:, _bucket(jnp.zeros((1,), jnp.int32))]),
        "tosel": _to_sel(n_sub, past // SEL_BLOCK),
    }


def _pick_heads(o, hi_all):
    parts = []
    for h in range(4):
        lo = HEAD_DIM if (hi_all or h >= 2) else 0
        parts.append(o[:, h, lo:lo + HEAD_DIM])
    return jnp.concatenate(parts, axis=-1)


def _sample_mixers(xs, lw, tabs, cos_t, sin_t, layer, pt, cache_moba, cache_mla, cache_cmp, cache_sel, state):
    nseq = xs.shape[0]
    pr = _inproj(xs, lw, cos_t, sin_t, tm=nseq, nb=1)
    o_a = _moba_decode(pt, pr["aq"].reshape(nseq, 1, 256), pr["mkv"].reshape(nseq, 1, 256), tabs["brow_a"],
                       tabs["b0_a"], cache_moba, layer)
    o_lat = _mla_decode(pt, jnp.transpose(pr["q"], (1, 0, 2)), pr["kpad"].reshape(nseq, 1, 256), cache_mla, layer)
    o_b = _dense(o_lat.reshape(nseq, B_HEADS * 128), lw["wuv"].reshape(B_HEADS * 128, 512))
    g = pr["gate"][:, :3 * C_HEADS].reshape(nseq, C_HEADS, 3)
    g = jnp.pad(g, ((0, 0), (0, 8 - C_HEADS), (0, LANES - 3)))
    o_c, win_new = _nsa_decode(pt, pr["cq"].reshape(nseq, 1, 256), pr["sel"].reshape(nseq, 1, 128),
                               pr["win"].reshape(nseq, 1, 128), pr["win_t"][0], g, state, lw, tabs["bc"],
                               tabs["bs"], tabs["bw"], tabs["b0_c"], tabs["tosel"], cache_cmp, cache_sel, layer)
    return pr, _pick_heads(o_a, hi_all=False), o_b, _pick_heads(o_c, hi_all=True), win_new


def _leaf_views(pr, nb, s):
    tr = lambda a, *mid: jnp.moveaxis(a.reshape((nb,) + mid + (s,)), -1, 1)
    return {"moba": tr(pr["mkv_t"], 2, A_KV_HEADS, HEAD_DIM), "mla": tr(pr["mla_t"], B_KV_RANK + B_ROPE),
            "cmp": tr(pr["cmp_t"], 2, HEAD_DIM), "sel": tr(pr["sel_t"], 2, HEAD_DIM),
            "win": tr(pr["win_t"], 2, HEAD_DIM)}


def kernel(x_prompt, x_sample, cache_moba_kv, cache_mla, cache_nsa_cmp_kv, cache_nsa_sel_kv, state_nsa_win_kv,
           page_table, w_in, w_out, ffn_w_gate, ffn_w_up, ffn_w_down, ln_g, ln_b, mla_q_norm, mla_w_qb,
           mla_kv_norm, mla_w_uk, mla_w_uv, nsa_cmp_pos, nsa_cmp_w1, nsa_cmp_b1, nsa_cmp_w2, rel_bias):
    nb, s, _ = x_prompt.shape
    nseq, dec_seq, _ = x_sample.shape
    assert dec_seq == 1
    n_pool = cache_mla.shape[1]
    n_pages = page_table.shape[1]
    past = n_pages * PAGE_SIZE
    wbuf = state_nsa_win_kv.shape[2]
    wlen = min(WINDOW, s)
    pt = page_table.reshape(-1)
    cmoba = jnp.moveaxis(cache_moba_kv, 2, -1).reshape(DEPTH, n_pool, 256, PAGE_SIZE)
    cmla = jnp.moveaxis(cache_mla, 2, -1)
    ccmp = jnp.moveaxis(cache_nsa_cmp_kv, 2, -1).reshape(DEPTH, n_pool, 128, PAGE_SIZE)
    csel = jnp.moveaxis(cache_nsa_sel_kv, 2, -1).reshape(DEPTH, n_pool, 128, PAGE_SIZE)
    state = jnp.moveaxis(state_nsa_win_kv, 2, -1).reshape(DEPTH, nseq, 128, wbuf)
    ptabs = _prompt_tables(rel_bias, s)
    stabs = _sample_tables(rel_bias, past, wbuf)
    cos_p, sin_p = _rope_tables(jnp.arange(s, dtype=jnp.int32))
    cos_s, sin_s = _rope_tables(jnp.full((nseq,), past, dtype=jnp.int32))
    xp = x_prompt.reshape(nb * s, D_MODEL)
    xs = x_sample.reshape(nseq, D_MODEL)
    outs = {k: [] for k in ("moba_p", "moba_s", "mla_p", "mla_s", "cmp_p", "cmp_s", "sel_p", "sel_s", "win_p", "win_s")}
    for l in range(DEPTH):
        lw = _prep_layer(l, w_in, w_out, mla_q_norm, mla_w_qb, mla_kv_norm, mla_w_uk, mla_w_uv,
                         nsa_cmp_pos, nsa_cmp_w1, nsa_cmp_b1, nsa_cmp_w2)
        ffn = [(ffn_w_gate[l, i].astype(bf16), ffn_w_up[l, i].astype(bf16), ffn_w_down[l, i].astype(bf16))
               for i in range(2)]
        lng = lambda i: ln_g[l, i].reshape(1, D_MODEL)
        lnb = lambda i: ln_b[l, i].reshape(1, D_MODEL)
        xp = _ffn(xp, *ffn[0], lng(0), lnb(0), tm=512)
        xs = _ffn(xs, *ffn[0], lng(0), lnb(0), tm=nseq)
        pr, o_a, o_b, o_c = _prompt_mixers(xp, lw, ptabs, cos_p, sin_p, nb, s)
        xp = _outproj(xp, o_a, o_b, o_c, lw["w_out"], lng(1), lnb(1), tm=512)
        lv = _leaf_views(pr, nb, s)
        for k in ("moba", "mla", "cmp", "sel"):
            outs[k + "_p"].append(lv[k])
        outs["win_p"].append(lv["win"][:, s - wlen:])
        pr, o_a, o_b, o_c, win_new = _sample_mixers(xs, lw, stabs, cos_s, sin_s, l, pt, cmoba, cmla, ccmp,
                                                    csel, state)
        xs = _outproj(xs, o_a, o_b, o_c, lw["w_out"], lng(1), lnb(1), tm=nseq)
        lv = _leaf_views(pr, 1, nseq)
        for k in ("moba", "mla", "cmp", "sel"):
            outs[k + "_s"].append(lv[k].reshape((nseq, 1) + lv[k].shape[2:]))
        outs["win_s"].append(jnp.moveaxis(win_new.reshape(nseq, 2, HEAD_DIM, wbuf), -1, 1))
        xp = _ffn(xp, *ffn[1], lng(2), lnb(2), tm=512)
        xs = _ffn(xs, *ffn[1], lng(2), lnb(2), tm=nseq)
    st = lambda k: jnp.stack(outs[k])
    return (xp.reshape(nb, s, D_MODEL), xs.reshape(nseq, 1, D_MODEL), st("moba_p"), st("moba_s"), st("mla_p"),
            st("mla_s"), st("cmp_p"), st("cmp_s"), st("sel_p"), st("sel_s"), st("win_p"), st("win_s"))
```

```python
import functools
import math

import numpy as np
import jax
import jax.numpy as jnp
from jax import lax
from jax.experimental import pallas as pl
from jax.experimental.pallas import tpu as pltpu

f32 = jnp.float32
bf16 = jnp.bfloat16

D_MODEL = 1024
DEPTH = 4
PAGE_SIZE = 128
HEAD_DIM = 64
A_HEADS = 4
A_KV_HEADS = 2
MOBA_BLOCK = 256
MOBA_TOPK = 3
B_HEADS = 8
B_Q_RANK = 256
B_KV_RANK = 128
B_NOPE = 64
B_ROPE = 32
B_V = 64
ROPE_THETA = 10000.0
C_HEADS = 4
CMP_STRIDE = 16
CMP_BLOCK = 2 * CMP_STRIDE
CMP_HIDDEN = 128
SEL_BLOCK = 64
SEL_TOPK = 16
WINDOW = 512
NUM_BUCKETS = 32
MAX_DISTANCE = 128
D_FF = 2816
ALPHA = (2 * DEPTH) ** 0.25
FORCE_SCORE = 1e9
LN_EPS = 1e-5
RMS_EPS = 1e-6
NEG = -1e30

LANES = 128
VMEM_LIMIT = 56 * 1024 * 1024

IN_W = 1920


def _cparams(sem):
    return pltpu.CompilerParams(dimension_semantics=sem, vmem_limit_bytes=VMEM_LIMIT)


def _nt(a, b):
    return lax.dot_general(a, b, (((1,), (1,)), ((), ())), preferred_element_type=f32)


def _mm(a, b):
    return jnp.dot(a, b, preferred_element_type=f32)


def _sigmoid(x):
    return 1.0 / (1.0 + jnp.exp(-x))


def _layer_norm(y, g, b):
    mu = jnp.mean(y, axis=-1, keepdims=True)
    d = y - mu
    var = jnp.mean(d * d, axis=-1, keepdims=True)
    return d * lax.rsqrt(var + LN_EPS) * g + b


def _rms_norm(x, g):
    return x * lax.rsqrt(jnp.mean(x * x, axis=-1, keepdims=True) + RMS_EPS) * g


def _const_spec(shape):
    nd = len(shape)
    return pl.BlockSpec(shape, lambda *_: (0,) * nd, pipeline_mode=pl.Buffered(1))


def _ffn_kernel(x_ref, wg_ref, wu_ref, wd_ref, g_ref, b_ref, o_ref, *, tf):
    x = x_ref[...]
    xb = x.astype(bf16)
    acc = jnp.zeros(x.shape, f32)
    for c in range(D_FF // tf):
        sl = slice(c * tf, (c + 1) * tf)
        gt = _mm(xb, wg_ref[:, sl])
        up = _mm(xb, wu_ref[:, sl])
        h = (gt * _sigmoid(gt) * up).astype(bf16)
        acc = acc + _mm(h, wd_ref[sl, :])
    o_ref[...] = _layer_norm(ALPHA * x + 0.5 * acc, g_ref[...], b_ref[...])


def _ffn(x, wg, wu, wd, g, b, tm):
    n = x.shape[0]
    return pl.pallas_call(
        functools.partial(_ffn_kernel, tf=256),
        out_shape=jax.ShapeDtypeStruct((n, D_MODEL), f32),
        grid=(n // tm,),
        in_specs=[pl.BlockSpec((tm, D_MODEL), lambda i: (i, 0)),
                  _const_spec((D_MODEL, D_FF)), _const_spec((D_MODEL, D_FF)), _const_spec((D_FF, D_MODEL)),
                  _const_spec((1, D_MODEL)), _const_spec((1, D_MODEL))],
        out_specs=pl.BlockSpec((tm, D_MODEL), lambda i: (i, 0)),
        compiler_params=_cparams(("parallel",)),
        name="ffn_ln",
    )(x, wg, wu, wd, g, b)


def _outproj_kernel(x_ref, oa_ref, ob_ref, oc_ref, w_ref, g_ref, b_ref, o_ref):
    mix = (_mm(oa_ref[...].astype(bf16), w_ref[0:256, :])
           + _mm(ob_ref[...].astype(bf16), w_ref[256:768, :])
           + _mm(oc_ref[...].astype(bf16), w_ref[768:1024, :]))
    o_ref[...] = _layer_norm(ALPHA * x_ref[...] + mix, g_ref[...], b_ref[...])


def _outproj(x, oa, ob, oc, w, g, b, tm):
    n = x.shape[0]
    row = lambda c: pl.BlockSpec((tm, c), lambda i: (i, 0))
    return pl.pallas_call(
        _outproj_kernel,
        out_shape=jax.ShapeDtypeStruct((n, D_MODEL), f32),
        grid=(n // tm,),
        in_specs=[row(D_MODEL), row(256), row(512), row(256), _const_spec((1024, D_MODEL)),
                  _const_spec((1, D_MODEL)), _const_spec((1, D_MODEL))],
        out_specs=row(D_MODEL),
        compiler_params=_cparams(("parallel",)),
        name="outproj_ln",
    )(x, oa, ob, oc, w, g, b)


def _inproj_kernel(x_ref, w_ref, wn_ref, wr_ref, wrs_ref, wuk_ref, qg_ref, kg_ref, cos_ref, sin_ref,
                   aq_ref, mkv_ref, q_ref, kpad_ref, cq_ref, cmp_ref, sel_ref, win_ref, gate_ref,
                   selb_ref, winb_ref, mkvb_ref, kmean_ref, mkvt_ref, mlat_ref, cmpt_ref, selt_ref, wint_ref):
    y = _mm(x_ref[...].astype(bf16), w_ref[...])
    tm = y.shape[0]
    aq_ref[...] = y[:, 0:256].astype(bf16)
    mkv = y[:, 256:512]
    mkv_ref[...] = mkv
    mkvb_ref[...] = mkv.astype(bf16)
    mkvt_ref[0] = mkv.T
    for n in range(kmean_ref.shape[1]):
        kmean_ref[0, n:n + 1, :] = jnp.mean(mkv[n * MOBA_BLOCK:(n + 1) * MOBA_BLOCK, 0:128], axis=0, keepdims=True)
    cosv = cos_ref[...]
    sinv = sin_ref[...]
    qn = _rms_norm(y[:, 512:768], qg_ref[...]).astype(bf16)
    nope = _mm(qn, wn_ref[...])
    qpe = _mm(qn, wr_ref[...]) * cosv + _mm(qn, wrs_ref[...]) * sinv
    lane = lax.broadcasted_iota(jnp.int32, (tm, LANES), 1)
    for j in range(B_HEADS // 2):
        ql = _mm(nope[:, 128 * j:128 * (j + 1)].astype(bf16), wuk_ref[j])
        for t in range(2):
            h = 2 * j + t
            q_ref[h, :, 0:128] = ql[:, 128 * t:128 * (t + 1)].astype(bf16)
            src = qpe[:, 128 * (h // 4):128 * (h // 4 + 1)]
            shift = (LANES - B_ROPE * (h % 4)) % LANES
            if shift:
                src = pltpu.roll(src, shift, 1)
            q_ref[h, :, 128:256] = jnp.where(lane < B_ROPE, src, 0.0).astype(bf16)
    ckv = _rms_norm(y[:, 768:896], kg_ref[...])
    kpe = y[:, 896:1024] * cosv[:, 0:128] + y[:, 1024:1152] * sinv[:, 0:128]
    kpad_ref[:, 0:128] = ckv.astype(bf16)
    kpad_ref[:, 128:256] = kpe.astype(bf16)
    mlat_ref[0, 0:128, :] = ckv.T
    mlat_ref[0, 128:160, :] = kpe.T[0:B_ROPE, :]
    cq_ref[...] = y[:, 1152:1408].astype(bf16)
    cmp = y[:, 1408:1536]
    sel = y[:, 1536:1664]
    win = y[:, 1664:1792]
    cmp_ref[...] = cmp
    sel_ref[...] = sel
    win_ref[...] = win
    selb_ref[...] = sel.astype(bf16)
    winb_ref[...] = win.astype(bf16)
    cmpt_ref[0] = cmp.T
    selt_ref[0] = sel.T
    wint_ref[0] = win.T
    gate_ref[...] = _sigmoid(y[:, 1792:1920])


def _inproj(x, lw, cos_t, sin_t, tm, nb):
    n = x.shape[0]
    s = n // nb
    spb = s // tm
    row = lambda c: pl.BlockSpec((tm, c), lambda i: (i, 0))
    tab = pl.BlockSpec((tm, 256), lambda i: (i % spb, 0))
    fmaj = lambda f: pl.BlockSpec((1, f, tm), lambda i: (i // spb, 0, i % spb))
    nkm = max(tm // MOBA_BLOCK, 1)
    sd = jax.ShapeDtypeStruct
    outs = [sd((n, 256), bf16), sd((n, 256), f32), sd((B_HEADS, n, 256), bf16), sd((n, 256), bf16),
            sd((n, 256), bf16), sd((n, 128), f32), sd((n, 128), f32), sd((n, 128), f32),
            sd((n, 128), f32), sd((n, 128), bf16), sd((n, 128), bf16), sd((n, 256), bf16),
            sd((n // tm, nkm, 128), f32), sd((nb, 256, s), f32), sd((nb, 160, s), f32), sd((nb, 128, s), f32),
            sd((nb, 128, s), f32), sd((nb, 128, s), f32)]
    out_specs = [row(256), row(256), pl.BlockSpec((B_HEADS, tm, 256), lambda i: (0, i, 0)), row(256),
                 row(256), row(128), row(128), row(128), row(128), row(128), row(128), row(256),
                 pl.BlockSpec((1, nkm, 128), lambda i: (i, 0, 0)), fmaj(256), fmaj(160), fmaj(128), fmaj(128), fmaj(128)]
    names = ("aq", "mkv", "q", "kpad", "cq", "cmp", "sel", "win", "gate", "selb", "winb", "mkvb", "kmean",
             "mkv_t", "mla_t", "cmp_t", "sel_t", "win_t")
    res = pl.pallas_call(
        _inproj_kernel,
        out_shape=outs,
        grid=(n // tm,),
        in_specs=[row(D_MODEL), _const_spec((D_MODEL, IN_W)), _const_spec((256, 512)), _const_spec((256, 256)),
                  _const_spec((256, 256)), _const_spec((4, 128, 256)), _const_spec((1, 256)), _const_spec((1, 128)),
                  tab, tab],
        out_specs=out_specs,
        compiler_params=_cparams(("parallel",)),
        name="in_proj",
    )(x, lw["w_in"], lw["wn"], lw["wr"], lw["wrs"], lw["wuk"], lw["qg"], lw["kg"], cos_t, sin_t)
    return dict(zip(names, res))


def _online_update(m_sc, l_sc, acc_sc, s, valid, v):
    m_prev = m_sc[...]
    m_new = jnp.maximum(m_prev, jnp.max(s, axis=1, keepdims=True))
    alpha = jnp.exp(m_prev - m_new)
    p = jnp.exp(s - m_new)
    if valid is not None:
        p = jnp.where(valid, p, 0.0)
    l_sc[...] = alpha * l_sc[...] + jnp.sum(p, axis=1, keepdims=True)
    acc_sc[...] = alpha * acc_sc[...] + _mm(p.astype(bf16), v)
    m_sc[...] = m_new


def _init_softmax(m_sc, l_sc, acc_sc):
    m_sc[...] = jnp.full(m_sc.shape, NEG, f32)
    l_sc[...] = jnp.zeros(l_sc.shape, f32)
    acc_sc[...] = jnp.zeros(acc_sc.shape, f32)


def _finish(l_sc, acc_sc):
    return acc_sc[...] * (1.0 / jnp.maximum(l_sc[...], 1e-30))


def _stack_heads_lo(qt, lo):
    t = qt.shape[0]
    lane = lax.broadcasted_iota(jnp.int32, (t, LANES), 1)
    a, b = qt[:, 0:128], qt[:, 128:256]
    low = lane < HEAD_DIM
    h0 = jnp.where(low, a, 0.0)
    h1 = jnp.where(low, pltpu.roll(a, HEAD_DIM, 1), 0.0)
    if lo:
        h2 = jnp.where(low, b, 0.0)
        h3 = jnp.where(low, pltpu.roll(b, HEAD_DIM, 1), 0.0)
    else:
        h2 = jnp.where(low, 0.0, pltpu.roll(b, HEAD_DIM, 1))
        h3 = jnp.where(low, 0.0, b)
    return jnp.concatenate([h0, h1, h2, h3], axis=0)


def _unstack_heads(o, hi_all):
    t = o.shape[0] // 4
    lane = lax.broadcasted_iota(jnp.int32, (t, LANES), 1)
    low = lane < HEAD_DIM
    o0, o1, o2, o3 = (o[i * t:(i + 1) * t] for i in range(4))
    if hi_all:
        left = jnp.where(low, pltpu.roll(o0, HEAD_DIM, 1), o1)
    else:
        left = jnp.where(low, o0, pltpu.roll(o1, HEAD_DIM, 1))
    right = jnp.where(low, pltpu.roll(o2, HEAD_DIM, 1), o3)
    return jnp.concatenate([left, right], axis=1)


def _topk_mask(vals, k, lanef):
    sel = jnp.zeros(vals.shape, f32)
    big = jnp.float32(1e9)
    for _ in range(k):
        mx = jnp.max(vals, axis=1, keepdims=True)
        idx = jnp.min(jnp.where(vals == mx, lanef, big), axis=1, keepdims=True)
        hit = lanef == idx
        sel = jnp.where(hit & (mx > -jnp.inf), 1.0, sel)
        vals = jnp.where(hit, -jnp.inf, vals)
    return sel


def _mla_prompt_kernel(q_ref, k_ref, wuv_ref, o_ref, m_sc, l_sc, acc_sc, *, tq, tk, scale):
    qi = pl.program_id(1)
    q = q_ref[...].reshape(B_HEADS * tq, 256)
    _init_softmax(m_sc, l_sc, acc_sc)
    nfull = (qi * tq) // tk

    def full_step(kt, c):
        k = k_ref[pl.ds(pl.multiple_of(kt * tk, tk), tk), :]
        _online_update(m_sc, l_sc, acc_sc, _nt(q, k) * scale, None, k[:, 0:128])
        return c

    lax.fori_loop(0, nfull, full_step, 0)
    k = k_ref[pl.ds(pl.multiple_of(nfull * tk, tk), tk), :]
    s = _nt(q, k) * scale
    rows = lax.broadcasted_iota(jnp.int32, s.shape, 0)
    cols = lax.broadcasted_iota(jnp.int32, s.shape, 1)
    valid = (nfull * tk + cols) <= (qi * tq + (rows & (tq - 1)))
    _online_update(m_sc, l_sc, acc_sc, jnp.where(valid, s, NEG), valid, k[:, 0:128])
    o = _finish(l_sc, acc_sc).astype(bf16)
    out = _mm(o[0:tq], wuv_ref[0])
    for h in range(1, B_HEADS):
        out = out + _mm(o[h * tq:(h + 1) * tq], wuv_ref[h])
    o_ref[...] = out


def _mla_prompt(q, kpad, wuv, nb, s, tq=128, tk=2048):
    tk = min(tk, s)
    assert s % tk == 0 and tk % tq == 0
    nq = s // tq
    r = B_HEADS * tq
    return pl.pallas_call(
        functools.partial(_mla_prompt_kernel, tq=tq, tk=tk, scale=(B_NOPE + B_ROPE) ** -0.5),
        out_shape=jax.ShapeDtypeStruct((nb * s, 512), f32),
        grid=(nb, nq),
        in_specs=[pl.BlockSpec((B_HEADS, tq, 256), lambda b, i: (0, b * nq + i, 0)),
                  pl.BlockSpec((s, 256), lambda b, i: (b, 0), pipeline_mode=pl.Buffered(1)),
                  _const_spec((B_HEADS, 128, 512))],
        out_specs=pl.BlockSpec((tq, 512), lambda b, i: (b * nq + i, 0)),
        scratch_shapes=[pltpu.VMEM((r, 1), f32), pltpu.VMEM((r, 1), f32), pltpu.VMEM((r, 128), f32)],
        compiler_params=_cparams(("parallel", "arbitrary")),
        name="mla_prompt",
    )(q, kpad, wuv)


def _moba_prompt_kernel(aq_ref, kv_ref, kmean_ref, bias_ref, o_ref, m_sc, l_sc, acc_sc, *, tq, nblk, gb, scale):
    qi = pl.program_id(1)
    blk = MOBA_BLOCK
    qpad = _stack_heads_lo(aq_ref[...].astype(f32), lo=False).astype(bf16)
    r = 4 * tq
    own = (qi * tq) // blk
    lanef = lax.broadcasted_iota(jnp.int32, (r, nblk), 1).astype(f32)
    ownf = own.astype(f32)
    gate = _nt(qpad, kmean_ref[0].astype(bf16))
    gate = jnp.where(lanef < ownf, gate, -jnp.inf)
    sel = _topk_mask(gate, min(MOBA_TOPK, nblk), lanef)
    selb = sel.astype(bf16)
    _init_softmax(m_sc, l_sc, acc_sc)
    gk = gb * blk
    erow = lax.broadcasted_iota(jnp.int32, (nblk, gk), 0)
    ecol = lax.broadcasted_iota(jnp.int32, (nblk, gk), 1) // blk

    def far_step(g, c):
        kt = kv_ref[pl.ds(pl.multiple_of(g * gk, gk), gk), :]
        bidx = g * gb + ecol
        expand = ((erow == bidx) & (bidx < own - 1)).astype(bf16)
        hit = _mm(selb, expand) > 0.5
        _online_update(m_sc, l_sc, acc_sc, jnp.where(hit, _nt(qpad, kt[:, 0:128]) * scale, NEG), hit, kt[:, 128:256])
        return c

    lax.fori_loop(0, (jnp.maximum(own - 1, 0) + gb - 1) // gb, far_step, 0)
    t0 = jnp.maximum(own - 1, 0)
    kt = kv_ref[pl.ds(pl.multiple_of(t0 * blk, blk), 2 * blk), :]
    s = _nt(qpad, kt[:, 0:128]) * scale + bias_ref[(qi * tq - t0 * blk) // 128]
    rows = lax.broadcasted_iota(jnp.int32, s.shape, 0)
    cols = lax.broadcasted_iota(jnp.int32, s.shape, 1)
    picked_prev = jnp.sum(jnp.where(lanef == ownf - 1.0, sel, 0.0), axis=1, keepdims=True)
    kpos = t0 * blk + cols
    in_own = kpos >= own * blk
    valid = ((in_own & (kpos <= qi * tq + (rows & (tq - 1))))
             | (jnp.logical_not(in_own) & (jnp.broadcast_to(picked_prev, s.shape) > 0.5)))
    _online_update(m_sc, l_sc, acc_sc, jnp.where(valid, s, NEG), valid, kt[:, 128:256])
    o_ref[...] = _unstack_heads(_finish(l_sc, acc_sc), hi_all=False)


def _moba_prompt(aq, mkvb, kmean, bias, nb, s, tq=128, gb=8):
    nq = s // tq
    nblk = s // MOBA_BLOCK
    gb = min(gb, nblk)
    assert s % MOBA_BLOCK == 0 and nblk % gb == 0 and nblk >= 2
    r = 4 * tq
    return pl.pallas_call(
        functools.partial(_moba_prompt_kernel, tq=tq, nblk=nblk, gb=gb, scale=HEAD_DIM ** -0.5),
        out_shape=jax.ShapeDtypeStruct((nb * s, 256), f32),
        grid=(nb, nq),
        in_specs=[pl.BlockSpec((tq, 256), lambda b, i: (b * nq + i, 0)),
                  pl.BlockSpec((s, 256), lambda b, i: (b, 0), pipeline_mode=pl.Buffered(1)),
                  pl.BlockSpec((1, nblk, 128), lambda b, i: (b, 0, 0)),
                  _const_spec((4, r, 2 * MOBA_BLOCK))],
        out_specs=pl.BlockSpec((tq, 256), lambda b, i: (b * nq + i, 0)),
        scratch_shapes=[pltpu.VMEM((r, 1), f32), pltpu.VMEM((r, 1), f32), pltpu.VMEM((r, 128), f32)],
        compiler_params=_cparams(("parallel", "arbitrary")),
        name="moba_prompt",
    )(aq, mkvb, kmean, bias)


def _compress_tail(ya, yb, b1, w2):
    n_sub = ya.shape[0]
    hid = ya + pltpu.roll(yb, n_sub - 1, 0) + b1
    act = (hid * _sigmoid(hid)).astype(bf16)
    ckv = _mm(act, w2)
    rows = lax.broadcasted_iota(jnp.int32, ckv.shape, 0)
    return jnp.where(rows < n_sub - 1, ckv, 0.0)


def _compress_kernel(x_ref, pea_ref, peb_ref, wa_ref, wb_ref, b1_ref, w2_ref, o_ref):
    x = x_ref[...]
    ya = _mm((x + pea_ref[...]).astype(bf16), wa_ref[...])
    yb = _mm((x + peb_ref[...]).astype(bf16), wb_ref[...])
    o_ref[...] = _compress_tail(ya, yb, b1_ref[...], w2_ref[...]).astype(bf16)


def _compress_prompt(xc, lw, nb, n_sub):
    return pl.pallas_call(
        _compress_kernel,
        out_shape=jax.ShapeDtypeStruct((nb * n_sub, 128), bf16),
        grid=(nb,),
        in_specs=[pl.BlockSpec((n_sub, 2048), lambda b: (b, 0)), _const_spec((1, 2048)), _const_spec((1, 2048)),
                  _const_spec((2048, 256)), _const_spec((2048, 256)), _const_spec((1, 256)), _const_spec((256, 128))],
        out_specs=pl.BlockSpec((n_sub, 128), lambda b: (b, 0)),
        compiler_params=_cparams(("parallel",)),
        name="nsa_compress",
    )(xc, lw["pea"], lw["peb"], lw["wa"], lw["wb"], lw["b1"], lw["w2"])


def _nsa_cmp_kernel(cq_ref, ckv_ref, pb_ref, tosel_ref, oc_ref, selm_ref, *, tq, n_sub, ns, scale):
    qi = pl.program_id(1)
    q0 = qi * tq
    qpad = _stack_heads_lo(cq_ref[...].astype(f32), lo=True).astype(bf16)
    ckv = ckv_ref[...]
    r = 4 * tq
    bias = pltpu.roll(pb_ref[...], q0 // CMP_STRIDE, 1)
    s = _nt(qpad, ckv) * scale + bias
    rows = lax.broadcasted_iota(jnp.int32, (r, n_sub), 0)
    cols = lax.broadcasted_iota(jnp.int32, (r, n_sub), 1)
    qpos = q0 + (rows & (tq - 1))
    valid = (cols * CMP_STRIDE + (CMP_BLOCK - 1)) <= qpos
    s = jnp.where(valid, s, NEG)
    m = jnp.max(s, axis=1, keepdims=True)
    e = jnp.where(valid, jnp.exp(s - m), 0.0)
    p = e * (1.0 / jnp.maximum(jnp.sum(e, axis=1, keepdims=True), 1e-30))
    pb = p.astype(bf16)
    oc_ref[...] = _unstack_heads(_mm(pb, ckv), hi_all=True)
    tosel = tosel_ref[...]
    imp = _mm(pb[0:tq], tosel)
    for h in range(1, C_HEADS):
        imp = imp + _mm(pb[h * tq:(h + 1) * tq], tosel)
    lanef = lax.broadcasted_iota(jnp.int32, imp.shape, 1).astype(f32)
    irow = lax.broadcasted_iota(jnp.int32, imp.shape, 0)
    cur = ((q0 + irow) // SEL_BLOCK).astype(f32)
    imp = jnp.where((lanef == 0.0) | (lanef == cur) | (lanef == cur - 1.0), FORCE_SCORE, imp)
    imp = jnp.where(lanef <= cur, imp, -jnp.inf)
    selm_ref[...] = _topk_mask(imp, min(SEL_TOPK, ns), lanef).astype(bf16)


def _nsa_cmp(cq, ckv, pbias, tosel, nb, s, tq=128):
    nq = s // tq
    n_sub = s // CMP_STRIDE
    ns = s // SEL_BLOCK
    nsp = tosel.shape[1]
    return pl.pallas_call(
        functools.partial(_nsa_cmp_kernel, tq=tq, n_sub=n_sub, ns=ns, scale=HEAD_DIM ** -0.5),
        out_shape=[jax.ShapeDtypeStruct((nb * s, 256), f32), jax.ShapeDtypeStruct((nb * s, nsp), bf16)],
        grid=(nb, nq),
        in_specs=[pl.BlockSpec((tq, 256), lambda b, i: (b * nq + i, 0)),
                  pl.BlockSpec((n_sub, 128), lambda b, i: (b, 0)),
                  _const_spec((4 * tq, n_sub)), _const_spec((n_sub, nsp))],
        out_specs=[pl.BlockSpec((tq, 256), lambda b, i: (b * nq + i, 0)),
                   pl.BlockSpec((tq, nsp), lambda b, i: (b * nq + i, 0))],
        compiler_params=_cparams(("parallel", "parallel")),
        name="nsa_cmp_select",
    )(cq, ckv, pbias, tosel)


def _nsa_main_kernel(cq_ref, sel_ref, win_ref, selm_ref, oc_ref, g_ref, bsel_ref, bwin_ref, o_ref,
                     m_sc, l_sc, acc_sc, *, tq, tk, nk, wk, scale):
    qi = pl.program_id(1)
    q0 = qi * tq
    r = 4 * tq
    qpad = _stack_heads_lo(cq_ref[...].astype(f32), lo=True).astype(bf16)
    selm = selm_ref[...]
    nsp = selm.shape[1]
    bper = tk // SEL_BLOCK
    _init_softmax(m_sc, l_sc, acc_sc)
    t0 = jnp.maximum(q0 + tq - nk, 0)

    def picked_mask(first_blk, width, limit):
        erow = lax.broadcasted_iota(jnp.int32, (nsp, width), 0)
        ecol = lax.broadcasted_iota(jnp.int32, (nsp, width), 1) // SEL_BLOCK + first_blk
        expand = ((erow == ecol) & (ecol < limit)).astype(bf16)
        hit = _mm(selm, expand) > 0.5
        return jnp.concatenate([hit] * 4, axis=0)

    def far_step(kt, c):
        k = sel_ref[pl.ds(pl.multiple_of(kt * tk, tk), tk), :]
        hit = picked_mask(kt * bper, tk, t0 // SEL_BLOCK)
        _online_update(m_sc, l_sc, acc_sc, jnp.where(hit, _nt(qpad, k) * scale, NEG), hit, k)
        return c

    lax.fori_loop(0, (t0 + tk - 1) // tk, far_step, 0)
    k = sel_ref[pl.ds(pl.multiple_of(t0, 128), nk), :]
    s = _nt(qpad, k) * scale + bsel_ref[(q0 - t0) // 128]
    rows = lax.broadcasted_iota(jnp.int32, s.shape, 0)
    cols = lax.broadcasted_iota(jnp.int32, s.shape, 1)
    qpos = q0 + (rows & (tq - 1))
    valid = picked_mask(t0 // SEL_BLOCK, nk, nsp) & ((t0 + cols) <= qpos)
    _online_update(m_sc, l_sc, acc_sc, jnp.where(valid, s, NEG), valid, k)
    o_s = _finish(l_sc, acc_sc)
    ks = jnp.maximum(q0 - WINDOW, 0)
    kw = win_ref[pl.ds(pl.multiple_of(ks, 128), wk), :]
    sw = _nt(qpad, kw) * scale + bwin_ref[(q0 - ks) // 128]
    rows = lax.broadcasted_iota(jnp.int32, sw.shape, 0)
    cols = lax.broadcasted_iota(jnp.int32, sw.shape, 1)
    dw = q0 + (rows & (tq - 1)) - (ks + cols)
    vw = (dw >= 0) & (dw <= WINDOW)
    sw = jnp.where(vw, sw, NEG)
    mw = jnp.max(sw, axis=1, keepdims=True)
    ew = jnp.where(vw, jnp.exp(sw - mw), 0.0)
    lw = jnp.maximum(jnp.sum(ew, axis=1, keepdims=True), 1e-30)
    o_w = _mm(ew.astype(bf16), kw) * (1.0 / lw)
    g = g_ref[...]
    os4 = _unstack_heads(o_s, hi_all=True)
    ow4 = _unstack_heads(o_w, hi_all=True)
    oc4 = oc_ref[...]
    lane = lax.broadcasted_iota(jnp.int32, (tq, 256), 1) // HEAD_DIM
    gc = jnp.zeros((tq, 256), f32)
    gs = jnp.zeros((tq, 256), f32)
    gw = jnp.zeros((tq, 256), f32)
    for h in range(C_HEADS):
        gc = jnp.where(lane == h, g[:, 3 * h:3 * h + 1], gc)
        gs = jnp.where(lane == h, g[:, 3 * h + 1:3 * h + 2], gs)
        gw = jnp.where(lane == h, g[:, 3 * h + 2:3 * h + 3], gw)
    o_ref[...] = gc * oc4 + gs * os4 + gw * ow4


def _nsa_main(cq, selb, winb, selm, oc, gate, bsel, bwin, nb, s, tq=128, tk=1024):
    tk = min(tk, s)
    nk = bsel.shape[2]
    assert s % tk == 0 and s >= nk and nk % 128 == 0
    nq = s // tq
    nsp = selm.shape[1]
    wk = WINDOW + tq
    r = 4 * tq
    qrow = lambda c: pl.BlockSpec((tq, c), lambda b, i: (b * nq + i, 0))
    seq = lambda c: pl.BlockSpec((s, c), lambda b, i: (b, 0), pipeline_mode=pl.Buffered(1))
    return pl.pallas_call(
        functools.partial(_nsa_main_kernel, tq=tq, tk=tk, nk=nk, wk=wk, scale=HEAD_DIM ** -0.5),
        out_shape=jax.ShapeDtypeStruct((nb * s, 256), f32),
        grid=(nb, nq),
        in_specs=[qrow(256), seq(128), seq(128), qrow(nsp), qrow(256), qrow(128),
                  _const_spec(bsel.shape), _const_spec(bwin.shape)],
        out_specs=qrow(256),
        scratch_shapes=[pltpu.VMEM((r, 1), f32), pltpu.VMEM((r, 1), f32), pltpu.VMEM((r, 128), f32)],
        compiler_params=_cparams(("parallel", "arbitrary")),
        name="nsa_sel_win",
    )(cq, selb, winb, selm, oc, gate, bsel, bwin)


def _bucket(dist):
    n = jnp.maximum(dist, 0)
    max_exact = NUM_BUCKETS // 2
    nf = jnp.maximum(n, 1).astype(f32)
    large = max_exact + (jnp.log(nf / max_exact) / math.log(MAX_DISTANCE / max_exact)
                         * (NUM_BUCKETS - max_exact)).astype(jnp.int32)
    return jnp.where(n < max_exact, n, jnp.minimum(large, NUM_BUCKETS - 1))


def _bias_by_dist(tab, dmax):
    return tab[:, _bucket(jnp.arange(dmax + 1, dtype=jnp.int32))]


def _toeplitz(bd, deltas, tq, tk, far, minus_far=False):
    h, d1 = bd.shape
    m = tk + tq
    k = np.arange(m)
    fconst = bd[:, d1 - 1][:, None, None]
    tiles = []
    for d in deltas:
        dist = np.where(k < tk, d - k, d + m - k)
        w = bd[:, np.clip(dist, 0, d1 - 1)]
        t = jnp.tile(w, (1, tq))[:, :tq * (m - 1)].reshape(h, tq, m - 1)[:, :, :tk]
        tiles.append(((t - fconst) if minus_far else t).reshape(h * tq, tk))
    if far:
        tiles.append(jnp.broadcast_to(fconst, (h, tq, tk)).reshape(h * tq, tk))
    return jnp.stack(tiles)


def _cmp_bias_table(bd, tq, n_sub):
    h, d1 = bd.shape
    lo, hi = 24, 8
    assert n_sub >= lo + hi
    m = np.concatenate([np.arange(hi), np.arange(-lo, 0)])
    dist = np.arange(tq)[:, None] - CMP_STRIDE * m[None, :] - (CMP_BLOCK - 1)
    near = bd[:, np.clip(dist, 0, d1 - 1)]
    far = jnp.broadcast_to(bd[:, d1 - 1][:, None, None], (h, tq, n_sub - lo - hi))
    return jnp.concatenate([near[:, :, :hi], far, near[:, :, hi:]], axis=2).reshape(h * tq, n_sub)


def _to_sel(n_sub, ns):
    nsp = -(-ns // LANES) * LANES
    i = np.arange(n_sub)[:, None] * CMP_STRIDE
    j = np.arange(nsp)[None, :] * SEL_BLOCK
    t = (i < j + SEL_BLOCK) & (i + CMP_BLOCK > j) & (np.arange(nsp)[None, :] < ns) & (np.arange(n_sub)[:, None] < n_sub - 1)
    return jnp.asarray(t, dtype=bf16)


def _rope_tables(pos):
    half = B_ROPE // 2
    inv = jnp.power(ROPE_THETA, -jnp.arange(half, dtype=f32) / half)
    ang = pos.astype(f32)[:, None] * inv
    c, s = jnp.cos(ang), jnp.sin(ang)
    return (jnp.tile(jnp.concatenate([c, c], 1), (1, B_HEADS)), jnp.tile(jnp.concatenate([-s, s], 1), (1, B_HEADS)))


def _prep_layer(l, w_in, w_out, mla_q_norm, mla_w_qb, mla_kv_norm, mla_w_uk, mla_w_uv,
                nsa_cmp_pos, nsa_cmp_w1, nsa_cmp_b1, nsa_cmp_w2):
    w = w_in[l]
    z = lambda n: jnp.zeros((D_MODEL, n), f32)
    w_in_p = jnp.concatenate([w[:, 0:928], z(96), w[:, 912:928], w[:, 896:912], z(96), w[:, 928:1568],
                              w[:, 1568:1580], z(116)], axis=1)
    qb = mla_w_qb[l]
    ukt = jnp.transpose(mla_w_uk[l], (1, 2, 0))
    z4 = jnp.zeros((4, 64, 128), f32)
    wuk = jnp.concatenate([jnp.concatenate([ukt[0::2], z4], axis=2), jnp.concatenate([z4, ukt[1::2]], axis=2)], axis=1)
    wuv = jnp.einsum('rhd,hg->hrgd', mla_w_uv[l], jnp.eye(B_HEADS, dtype=f32)).reshape(B_HEADS, 128, 512)
    pe = nsa_cmp_pos[l].reshape(2, 2, CMP_STRIDE, HEAD_DIM)
    w1r = nsa_cmp_w1[l].reshape(2, 2, CMP_STRIDE, HEAD_DIM, CMP_HIDDEN)
    eye2 = jnp.eye(2, dtype=f32)
    expand = lambda half: jnp.einsum('ksdj,kc->skdcj', w1r[:, half], eye2).reshape(2048, 256)
    return {
        "w_in": w_in_p.astype(bf16),
        "w_out": w_out[l].astype(bf16),
        "wn": qb[:, :, :B_NOPE].reshape(256, 512).astype(bf16),
        "wr": qb[:, :, B_NOPE:].reshape(256, 256).astype(bf16),
        "wrs": jnp.concatenate([qb[:, :, 80:96], qb[:, :, 64:80]], axis=-1).reshape(256, 256).astype(bf16),
        "wuk": wuk.astype(bf16),
        "wuv": wuv.astype(bf16),
        "qg": mla_q_norm[l].reshape(1, 256),
        "kg": mla_kv_norm[l].reshape(1, 128),
        "pea": jnp.transpose(pe[:, 0], (1, 0, 2)).reshape(1, 2048),
        "peb": jnp.transpose(pe[:, 1], (1, 0, 2)).reshape(1, 2048),
        "wa": expand(0).astype(bf16),
        "wb": expand(1).astype(bf16),
        "b1": nsa_cmp_b1[l].reshape(1, 256),
        "w2": jnp.einsum('kjd,kc->kjcd', nsa_cmp_w2[l], eye2).reshape(256, 128).astype(bf16),
    }


def _prompt_tables(rel_bias, s, tq=128):
    tab_a = rel_bias[:, :A_HEADS].T
    tab_c = rel_bias[:, A_HEADS:].T
    bd_a = _bias_by_dist(tab_a, 1024)
    bd_c = _bias_by_dist(tab_c, 2048)
    n_sub = s // CMP_STRIDE
    return {
        "moba": _toeplitz(bd_a, (0, 128, 256, 384), tq, 2 * MOBA_BLOCK, far=False, minus_far=True),
        "bsel": _toeplitz(bd_c, (0, 128, 256, 384), tq, 512, far=False, minus_far=True),
        "bwin": _toeplitz(bd_c, (0, 128, 256, 384, 512), tq, WINDOW + tq, far=False),
        "pcmp": _cmp_bias_table(bd_c, tq, n_sub),
        "tosel": _to_sel(n_sub, s // SEL_BLOCK),
    }


def _prompt_mixers(xp, lw, tabs, cos_t, sin_t, nb, s):
    pr = _inproj(xp, lw, cos_t, sin_t, tm=512, nb=nb)
    o_a = _moba_prompt(pr["aq"], pr["mkvb"], pr["kmean"].reshape(nb, s // MOBA_BLOCK, 128), tabs["moba"], nb, s)
    o_b = _mla_prompt(pr["q"], pr["kpad"], lw["wuv"], nb, s)
    ckv = _compress_prompt(pr["cmp"].reshape(nb * s // CMP_STRIDE, 2048), lw, nb, s // CMP_STRIDE)
    oc, selm = _nsa_cmp(pr["cq"], ckv, tabs["pcmp"], tabs["tosel"], nb, s)
    o_c = _nsa_main(pr["cq"], pr["selb"], pr["winb"], selm, oc, pr["gate"], tabs["bsel"], tabs["bwin"], nb, s)
    return pr, o_a, o_b, o_c


def _page_copy(pt_ref, cache, layer, buf, sem, seq, slot, p, n_pages):
    page = pt_ref[seq * n_pages + p]
    return pltpu.make_async_copy(cache.at[layer, page], buf.at[slot, :, pl.ds(p * PAGE_SIZE, PAGE_SIZE)],
                                 sem.at[slot])


def _prefetch_pages(pt_ref, cache, layer, buf, sem, n_pages, nseq):
    b = pl.program_id(0)
    slot = b % 2

    @pl.when(b == 0)
    def _():
        for p in range(n_pages):
            _page_copy(pt_ref, cache, layer, buf, sem, 0, 0, p, n_pages).start()

    @pl.when(b + 1 < nseq)
    def _():
        for p in range(n_pages):
            _page_copy(pt_ref, cache, layer, buf, sem, b + 1, 1 - slot, p, n_pages).start()

    for p in range(n_pages):
        _page_copy(pt_ref, cache, layer, buf, sem, b, slot, p, n_pages).wait()
    return slot


def _softmax_chunk(m, l, acc, s, hit, vt):
    m_new = jnp.maximum(m, jnp.max(s, axis=1, keepdims=True))
    alpha = jnp.exp(m - m_new)
    p = jnp.exp(s - m_new)
    if hit is not None:
        p = jnp.where(hit, p, 0.0)
    return m_new, alpha * l + jnp.sum(p, axis=1, keepdims=True), alpha * acc + _nt(p.astype(bf16), vt)


def _bf16_round(x):
    return x.astype(bf16).astype(f32)


def _mla_decode_kernel(pt_ref, q_ref, kn_ref, cache, o_ref, buf, sem, *, layer, n_pages, nseq, chunk, scale):
    slot = _prefetch_pages(pt_ref, cache, layer, buf, sem, n_pages, nseq)
    q = q_ref[0]
    knf = kn_ref[0].astype(f32)
    m = jnp.sum(q.astype(f32) * knf, axis=1, keepdims=True) * scale
    l = jnp.ones((B_HEADS, 1), f32)
    acc = jnp.broadcast_to(knf[:, 0:128], (B_HEADS, 128))
    q160 = q[:, 0:B_KV_RANK + B_ROPE]
    for c in range(n_pages * PAGE_SIZE // chunk):
        kc = buf[slot, :, c * chunk:(c + 1) * chunk].astype(bf16)
        m, l, acc = _softmax_chunk(m, l, acc, _mm(q160, kc) * scale, None, kc[0:128, :])
    o_ref[0] = acc * (1.0 / l)


def _mla_decode(pt, q, kn, cache, layer, chunk=8192):
    nseq = q.shape[0]
    n_pages = pt.shape[0] // nseq
    chunk = min(chunk, n_pages * PAGE_SIZE)
    width = cache.shape[2]
    return pl.pallas_call(
        functools.partial(_mla_decode_kernel, layer=layer, n_pages=n_pages, nseq=nseq, chunk=chunk,
                          scale=(B_NOPE + B_ROPE) ** -0.5),
        out_shape=jax.ShapeDtypeStruct((nseq, B_HEADS, 128), f32),
        grid_spec=pltpu.PrefetchScalarGridSpec(
            num_scalar_prefetch=1, grid=(nseq,),
            in_specs=[pl.BlockSpec((1, B_HEADS, 256), lambda b, pt: (b, 0, 0)),
                      pl.BlockSpec((1, 1, 256), lambda b, pt: (b, 0, 0)),
                      pl.BlockSpec(memory_space=pl.ANY)],
            out_specs=pl.BlockSpec((1, B_HEADS, 128), lambda b, pt: (b, 0, 0)),
            scratch_shapes=[pltpu.VMEM((2, width, n_pages * PAGE_SIZE), f32), pltpu.SemaphoreType.DMA((2,))]),
        compiler_params=_cparams(("arbitrary",)),
        name="mla_decode",
    )(pt, q, kn, cache)


def _decode_qrows(qt, lo):
    return jnp.concatenate([_stack_heads_lo(qt, lo), jnp.zeros((4, LANES), f32)], axis=0)


def _moba_decode_kernel(pt_ref, aq_ref, kvn_ref, brow_ref, b0_ref, cache, o_ref, buf, sem,
                        *, layer, n_pages, nseq, chunk, scale):
    slot = _prefetch_pages(pt_ref, cache, layer, buf, sem, n_pages, nseq)
    past = n_pages * PAGE_SIZE
    nblk = past // MOBA_BLOCK
    qb = _decode_qrows(aq_ref[0].astype(f32), lo=False).astype(bf16)
    qf = qb.astype(f32)
    lane_k = lax.broadcasted_iota(jnp.int32, (128, LANES), 1)
    kmean_t = jnp.zeros((128, LANES), f32)
    for n in range(nblk):
        col = jnp.sum(buf[slot, 0:128, n * MOBA_BLOCK:(n + 1) * MOBA_BLOCK], axis=1, keepdims=True) * (1.0 / MOBA_BLOCK)
        kmean_t = jnp.where(lane_k == n, col, kmean_t)
    lanef = lax.broadcasted_iota(jnp.int32, (8, LANES), 1).astype(f32)
    gate = jnp.where(lanef < nblk, _mm(qb, kmean_t.astype(bf16)), -jnp.inf)
    selb = _topk_mask(gate, min(MOBA_TOPK, nblk), lanef).astype(bf16)
    kvn = kvn_ref[0]
    m = jnp.sum(qf * _bf16_round(kvn[:, 0:128]), axis=1, keepdims=True) * scale + b0_ref[...]
    l = jnp.ones((8, 1), f32)
    acc = jnp.broadcast_to(_bf16_round(kvn[:, 128:256]), (8, 128))
    bper = chunk // MOBA_BLOCK
    erow = lax.broadcasted_iota(jnp.int32, (LANES, chunk), 0)
    ecol = lax.broadcasted_iota(jnp.int32, (LANES, chunk), 1) // MOBA_BLOCK
    for c in range(past // chunk):
        kt = buf[slot, :, c * chunk:(c + 1) * chunk]
        s = _mm(qb, kt[0:128, :].astype(bf16)) * scale + brow_ref[:, c * chunk:(c + 1) * chunk]
        hit = _mm(selb, (erow == c * bper + ecol).astype(bf16)) > 0.5
        m, l, acc = _softmax_chunk(m, l, acc, jnp.where(hit, s, NEG), hit, kt[128:256, :].astype(bf16))
    o_ref[0] = acc * (1.0 / l)


def _moba_decode(pt, aq, kvn, brow, b0, cache, layer, chunk=8192):
    nseq = aq.shape[0]
    n_pages = pt.shape[0] // nseq
    past = n_pages * PAGE_SIZE
    chunk = min(chunk, past)
    assert past % MOBA_BLOCK == 0 and past // MOBA_BLOCK <= LANES
    return pl.pallas_call(
        functools.partial(_moba_decode_kernel, layer=layer, n_pages=n_pages, nseq=nseq, chunk=chunk,
                          scale=HEAD_DIM ** -0.5),
        out_shape=jax.ShapeDtypeStruct((nseq, 8, 128), f32),
        grid_spec=pltpu.PrefetchScalarGridSpec(
            num_scalar_prefetch=1, grid=(nseq,),
            in_specs=[pl.BlockSpec((1, 1, 256), lambda b, pt: (b, 0, 0)),
                      pl.BlockSpec((1, 1, 256), lambda b, pt: (b, 0, 0)),
                      pl.BlockSpec((8, past), lambda b, pt: (0, 0)),
                      pl.BlockSpec((8, 1), lambda b, pt: (0, 0)),
                      pl.BlockSpec(memory_space=pl.ANY)],
            out_specs=pl.BlockSpec((1, 8, 128), lambda b, pt: (b, 0, 0)),
            scratch_shapes=[pltpu.VMEM((2, 256, past), f32), pltpu.SemaphoreType.DMA((2,))]),
        compiler_params=_cparams(("arbitrary",)),
        name="moba_decode",
    )(pt, aq, kvn, brow, b0, cache)


def _nsa_decode_kernel(pt_ref, cq_ref, seln_ref, winn_ref, wint_ref, g_ref, st_ref, pea_ref, peb_ref, wa_ref,
                       wb_ref, b1_ref, w2_ref, bc_ref, bs_ref, bw_ref, b0_ref, tosel_ref, cmp_cache, sel_cache,
                       o_ref, wout_ref, xbuf, sbuf, tok, semx, sems, *, layer, n_pages, nseq, chunk, scale):
    slot = _prefetch_pages(pt_ref, cmp_cache, layer, xbuf, semx, n_pages, nseq)
    _prefetch_pages(pt_ref, sel_cache, layer, sbuf, sems, n_pages, nseq)
    past = n_pages * PAGE_SIZE
    n_sub = past // CMP_STRIDE
    ns = past // SEL_BLOCK
    qb = _decode_qrows(cq_ref[0].astype(f32), lo=True).astype(bf16)
    qf = qb.astype(f32)
    b0 = b0_ref[...]
    for c in range(past // chunk):
        tok[c * chunk:(c + 1) * chunk, :] = xbuf[slot, :, c * chunk:(c + 1) * chunk].T
    ya = jnp.zeros((n_sub, 2 * CMP_HIDDEN), f32)
    yb = jnp.zeros((n_sub, 2 * CMP_HIDDEN), f32)
    for t in range(CMP_STRIDE):
        rows_t = tok[pl.ds(t, n_sub, stride=CMP_STRIDE), :]
        ya = ya + _mm((rows_t + pea_ref[t:t + 1, :]).astype(bf16), wa_ref[t])
        yb = yb + _mm((rows_t + peb_ref[t:t + 1, :]).astype(bf16), wb_ref[t])
    ckv = _compress_tail(ya, yb, b1_ref[...], w2_ref[...]).astype(bf16)
    s = _nt(qb, ckv) * scale + bc_ref[...]
    cols = lax.broadcasted_iota(jnp.int32, s.shape, 1)
    valid = (cols * CMP_STRIDE + (CMP_BLOCK - 1)) <= past
    s = jnp.where(valid, s, NEG)
    e = jnp.where(valid, jnp.exp(s - jnp.max(s, axis=1, keepdims=True)), 0.0)
    pb = (e * (1.0 / jnp.maximum(jnp.sum(e, axis=1, keepdims=True), 1e-30))).astype(bf16)
    o_c = _mm(pb, ckv)
    imp = jnp.sum(_mm(pb, tosel_ref[...])[0:C_HEADS], axis=0, keepdims=True)
    imp = jnp.broadcast_to(imp, (8, imp.shape[1]))
    lanef = lax.broadcasted_iota(jnp.int32, imp.shape, 1).astype(f32)
    imp = jnp.where((lanef == 0.0) | (lanef == ns - 1.0), FORCE_SCORE, imp)
    imp = jnp.where(lanef < ns, imp, -jnp.inf)
    selb = _topk_mask(imp, min(SEL_TOPK - 1, ns), lanef).astype(bf16)
    kn = _bf16_round(seln_ref[0])
    m = jnp.sum(qf * kn, axis=1, keepdims=True) * scale + b0
    l = jnp.ones((8, 1), f32)
    acc = jnp.broadcast_to(kn, (8, 128))
    nsp = selb.shape[1]
    bper = chunk // SEL_BLOCK
    erow = lax.broadcasted_iota(jnp.int32, (nsp, chunk), 0)
    ecol = lax.broadcasted_iota(jnp.int32, (nsp, chunk), 1) // SEL_BLOCK
    for c in range(past // chunk):
        kt = sbuf[slot, :, c * chunk:(c + 1) * chunk].astype(bf16)
        sc = _mm(qb, kt) * scale + bs_ref[:, c * chunk:(c + 1) * chunk]
        hit = _mm(selb, (erow == c * bper + ecol).astype(bf16)) > 0.5
        m, l, acc = _softmax_chunk(m, l, acc, jnp.where(hit, sc, NEG), hit, kt)
    o_s = acc * (1.0 / l)
    st = st_ref[0]
    wb = st.shape[1]
    stb = st.astype(bf16)
    wnr = _bf16_round(winn_ref[0])
    sw = _mm(qb, stb) * scale + bw_ref[...]
    sn = jnp.sum(qf * wnr, axis=1, keepdims=True) * scale + b0
    mw = jnp.maximum(jnp.max(sw, axis=1, keepdims=True), sn)
    ew = jnp.exp(sw - mw)
    en = jnp.exp(sn - mw)
    o_w = (_nt(ew.astype(bf16), stb) + en * wnr) * (1.0 / (jnp.sum(ew, axis=1, keepdims=True) + en))
    g = g_ref[0]
    o_ref[0] = g[:, 0:1] * o_c + g[:, 1:2] * o_s + g[:, 2:3] * o_w
    seq_lane = lax.broadcasted_iota(jnp.int32, wint_ref.shape, 1)
    wcol = jnp.sum(jnp.where(seq_lane == pl.program_id(0), wint_ref[...], 0.0), axis=1, keepdims=True)
    lanes = lax.broadcasted_iota(jnp.int32, st.shape, 1)
    wout_ref[0] = jnp.where(lanes == wb - 1, wcol, pltpu.roll(st, wb - 1, 1))


def _nsa_decode(pt, cq, seln, winn, wint, g, state, lw, bc, bs, bw, b0, tosel, cmp_cache, sel_cache, layer,
                chunk=8192):
    nseq = cq.shape[0]
    n_pages = pt.shape[0] // nseq
    past = n_pages * PAGE_SIZE
    chunk = min(chunk, past)
    n_sub = past // CMP_STRIDE
    wbuf = state.shape[3]
    nsp = tosel.shape[1]
    assert past % CMP_STRIDE == 0 and wbuf <= WINDOW
    seq3 = lambda r, c: pl.BlockSpec((1, r, c), lambda b, pt: (b, 0, 0))
    cst = lambda shape: pl.BlockSpec(shape, lambda b, pt: (0,) * len(shape), pipeline_mode=pl.Buffered(1))
    return pl.pallas_call(
        functools.partial(_nsa_decode_kernel, layer=layer, n_pages=n_pages, nseq=nseq, chunk=chunk,
                          scale=HEAD_DIM ** -0.5),
        out_shape=[jax.ShapeDtypeStruct((nseq, 8, 128), f32), jax.ShapeDtypeStruct((nseq, 128, wbuf), f32)],
        grid_spec=pltpu.PrefetchScalarGridSpec(
            num_scalar_prefetch=1, grid=(nseq,),
            in_specs=[seq3(1, 256), seq3(1, 128), seq3(1, 128), cst((128, nseq)), seq3(8, 128),
                      pl.BlockSpec((None, 1, 128, wbuf), lambda b, pt: (layer, b, 0, 0)),
                      cst((CMP_STRIDE, 128)), cst((CMP_STRIDE, 128)), cst((CMP_STRIDE, 128, 256)),
                      cst((CMP_STRIDE, 128, 256)), cst((1, 256)),
                      cst((256, 128)), cst((8, n_sub)), cst((8, past)), cst((8, wbuf)), cst((8, 1)),
                      cst((n_sub, nsp)),
                      pl.BlockSpec(memory_space=pl.ANY), pl.BlockSpec(memory_space=pl.ANY)],
            out_specs=[seq3(8, 128), seq3(128, wbuf)],
            scratch_shapes=[pltpu.VMEM((2, 128, past), f32), pltpu.VMEM((2, 128, past), f32),
                            pltpu.VMEM((past, 128), f32),
                            pltpu.SemaphoreType.DMA((2,)), pltpu.SemaphoreType.DMA((2,))]),
        compiler_params=_cparams(("arbitrary",)),
        name="nsa_decode",
    )(pt, cq, seln, winn, wint, g, state, lw["pea"].reshape(CMP_STRIDE, 128), lw["peb"].reshape(CMP_STRIDE, 128),
      lw["wa"].reshape(CMP_STRIDE, 128, 256), lw["wb"].reshape(CMP_STRIDE, 128, 256), lw["b1"], lw["w2"],
      bc, bs, bw, b0, tosel, cmp_cache, sel_cache)


def _dense_kernel(a_ref, w_ref, o_ref):
    o_ref[...] = _mm(a_ref[...].astype(bf16), w_ref[...])


def _dense(a, w):
    return pl.pallas_call(
        _dense_kernel,
        out_shape=jax.ShapeDtypeStruct((a.shape[0], w.shape[1]), f32),
        name="dense",
    )(a, w)


def _sample_tables(rel_bias, past, wbuf):
    tab_a = rel_bias[:, :A_HEADS].T
    tab_c = rel_bias[:, A_HEADS:].T
    pad = lambda t: jnp.concatenate([t, jnp.zeros_like(t)], axis=0)
    kpos = np.arange(past)
    n_sub = past // CMP_STRIDE
    c_end = np.arange(n_sub) * CMP_STRIDE + CMP_BLOCK - 1
    return {
        "brow_a": pad(tab_a[:, _bucket(jnp.asarray(past - kpos, jnp.int32))]),
        "b0_a": pad(tab_a[:, _bucket(jnp.zeros((1,), jnp.int32))]),
        "bc": pad(tab_c[:, _bucket(jnp.asarray(past - c_end, jnp.int32))]),
        "bs": pad(tab_c[:, _bucket(jnp.asarray(past - kpos, jnp.int32))]),
        "bw": pad(tab_c[:, _bucket(jnp.asarray(wbuf - np.arange(wbuf), jnp.int32))]),
        "b0_c": pad(tab_c[:, _bucket(jnp.zeros((1,), jnp.int32))]),
        "tosel": _to_sel(n_sub, past // SEL_BLOCK),
    }


def _pick_heads(o, hi_all):
    parts = []
    for h in range(4):
        lo = HEAD_DIM if (hi_all or h >= 2) else 0
        parts.append(o[:, h, lo:lo + HEAD_DIM])
    return jnp.concatenate(parts, axis=-1)


def _sample_mixers(xs, lw, tabs, cos_t, sin_t, layer, pt, cache_moba, cache_mla, cache_cmp, cache_sel, state):
    nseq = xs.shape[0]
    pr = _inproj(xs, lw, cos_t, sin_t, tm=nseq, nb=1)
    o_a = _moba_decode(pt, pr["aq"].reshape(nseq, 1, 256), pr["mkv"].reshape(nseq, 1, 256), tabs["brow_a"],
                       tabs["b0_a"], cache_moba, layer)
    o_lat = _mla_decode(pt, jnp.transpose(pr["q"], (1, 0, 2)), pr["kpad"].reshape(nseq, 1, 256), cache_mla, layer)
    o_b = _dense(o_lat.reshape(nseq, B_HEADS * 128), lw["wuv"].reshape(B_HEADS * 128, 512))
    g = pr["gate"][:, :3 * C_HEADS].reshape(nseq, C_HEADS, 3)
    g = jnp.pad(g, ((0, 0), (0, 8 - C_HEADS), (0, LANES - 3)))
    o_c, win_new = _nsa_decode(pt, pr["cq"].reshape(nseq, 1, 256), pr["sel"].reshape(nseq, 1, 128),
                               pr["win"].reshape(nseq, 1, 128), pr["win_t"][0], g, state, lw, tabs["bc"],
                               tabs["bs"], tabs["bw"], tabs["b0_c"], tabs["tosel"], cache_cmp, cache_sel, layer)
    return pr, _pick_heads(o_a, hi_all=False), o_b, _pick_heads(o_c, hi_all=True), win_new


def _leaf_views(pr, nb, s):
    tr = lambda a, *mid: jnp.moveaxis(a.reshape((nb,) + mid + (s,)), -1, 1)
    return {"moba": tr(pr["mkv_t"], 2, A_KV_HEADS, HEAD_DIM), "mla": tr(pr["mla_t"], B_KV_RANK + B_ROPE),
            "cmp": tr(pr["cmp_t"], 2, HEAD_DIM), "sel": tr(pr["sel_t"], 2, HEAD_DIM),
            "win": tr(pr["win_t"], 2, HEAD_DIM)}


def kernel(x_prompt, x_sample, cache_moba_kv, cache_mla, cache_nsa_cmp_kv, cache_nsa_sel_kv, state_nsa_win_kv,
           page_table, w_in, w_out, ffn_w_gate, ffn_w_up, ffn_w_down, ln_g, ln_b, mla_q_norm, mla_w_qb,
           mla_kv_norm, mla_w_uk, mla_w_uv, nsa_cmp_pos, nsa_cmp_w1, nsa_cmp_b1, nsa_cmp_w2, rel_bias):
    nb, s, _ = x_prompt.shape
    nseq, dec_seq, _ = x_sample.shape
    assert dec_seq == 1
    n_pool = cache_mla.shape[1]
    n_pages = page_table.shape[1]
    past = n_pages * PAGE_SIZE
    wbuf = state_nsa_win_kv.shape[2]
    wlen = min(WINDOW, s)
    pt = page_table.reshape(-1)
    cmoba = jnp.moveaxis(cache_moba_kv, 2, -1).reshape(DEPTH, n_pool, 256, PAGE_SIZE)
    cmla = jnp.moveaxis(cache_mla, 2, -1)
    ccmp = jnp.moveaxis(cache_nsa_cmp_kv, 2, -1).reshape(DEPTH, n_pool, 128, PAGE_SIZE)
    csel = jnp.moveaxis(cache_nsa_sel_kv, 2, -1).reshape(DEPTH, n_pool, 128, PAGE_SIZE)
    state = jnp.moveaxis(state_nsa_win_kv, 2, -1).reshape(DEPTH, nseq, 128, wbuf)
    ptabs = _prompt_tables(rel_bias, s)
    stabs = _sample_tables(rel_bias, past, wbuf)
    cos_p, sin_p = _rope_tables(jnp.arange(s, dtype=jnp.int32))
    cos_s, sin_s = _rope_tables(jnp.full((nseq,), past, dtype=jnp.int32))
    xp = x_prompt.reshape(nb * s, D_MODEL)
    xs = x_sample.reshape(nseq, D_MODEL)
    outs = {k: [] for k in ("moba_p", "moba_s", "mla_p", "mla_s", "cmp_p", "cmp_s", "sel_p", "sel_s", "win_p", "win_s")}
    for l in range(DEPTH):
        lw = _prep_layer(l, w_in, w_out, mla_q_norm, mla_w_qb, mla_kv_norm, mla_w_uk, mla_w_uv,
                         nsa_cmp_pos, nsa_cmp_w1, nsa_cmp_b1, nsa_cmp_w2)
        ffn = [(ffn_w_gate[l, i].astype(bf16), ffn_w_up[l, i].astype(bf16), ffn_w_down[l, i].astype(bf16))
               for i in range(2)]
        lng = lambda i: ln_g[l, i].reshape(1, D_MODEL)
        lnb = lambda i: ln_b[l, i].reshape(1, D_MODEL)
        xp = _ffn(xp, *ffn[0], lng(0), lnb(0), tm=512)
        xs = _ffn(xs, *ffn[0], lng(0), lnb(0), tm=nseq)
        pr, o_a, o_b, o_c = _prompt_mixers(xp, lw, ptabs, cos_p, sin_p, nb, s)
        xp = _outproj(xp, o_a, o_b, o_c, lw["w_out"], lng(1), lnb(1), tm=512)
        lv = _leaf_views(pr, nb, s)
        for k in ("moba", "mla", "cmp", "sel"):
            outs[k + "_p"].append(lv[k])
        outs["win_p"].append(lv["win"][:, s - wlen:])
        pr, o_a, o_b, o_c, win_new = _sample_mixers(xs, lw, stabs, cos_s, sin_s, l, pt, cmoba, cmla, ccmp,
                                                    csel, state)
        xs = _outproj(xs, o_a, o_b, o_c, lw["w_out"], lng(1), lnb(1), tm=nseq)
        lv = _leaf_views(pr, 1, nseq)
        for k in ("moba", "mla", "cmp", "sel"):
            outs[k + "_s"].append(lv[k].reshape((nseq, 1) + lv[k].shape[2:]))
        outs["win_s"].append(jnp.moveaxis(win_new.reshape(nseq, 2, HEAD_DIM, wbuf), -1, 1))
        xp = _ffn(xp, *ffn[1], lng(2), lnb(2), tm=512)
        xs = _ffn(xs, *ffn[1], lng(2), lnb(2), tm=nseq)
    st = lambda k: jnp.stack(outs[k])
    return (xp.reshape(nb, s, D_MODEL), xs.reshape(nseq, 1, D_MODEL), st("moba_p"), st("moba_s"), st("mla_p"),
            st("mla_s"), st("cmp_p"), st("cmp_s"), st("sel_p"), st("sel_s"), st("win_p"), st("win_s"))
```
